```python
import math
import jax
import jax.numpy as jnp
from jax import lax
import numpy as np

D_MODEL = 1024
BATCH = 16
SEQ = 2048
DEPTH = 4
DEC_BATCH = 2
DEC_SEQ = 8192
PAST_LEN = 128

N_MIXERS = 4
HEAD_DIM = 64
GRID_W = 64
Q_BLOCK = 128
ROPE_THETA = 10000.0
RMS_EPS = 1e-6
A_HEADS = D_MODEL // HEAD_DIM
A_KV_HEADS = 4
A_GROUP = A_HEADS // A_KV_HEADS
B_HEADS = D_MODEL // HEAD_DIM
B_Q_LORA = 384
B_KV_LORA = 256
B_NOPE = 64
B_ROPE = 32
B_V = 64
C_HEADS = D_MODEL // (2 * HEAD_DIM)
LAMBDA_INIT_BASE = 0.8
LAMBDA_INIT_AMP = 0.6
LAMBDA_INIT_RATE = 0.3
LAMBDA_STD = 0.1
D_HEADS = D_MODEL // HEAD_DIM
DILATED_BRANCHES = ((128, 1), (512, 4), (2048, 16))
N_EXPERTS = 16
EC_CAPACITY = 2
EXPERT_FF = 2 * D_MODEL

kernel_name = 'hybrid_bidir_encoder_ec'


def rmsnorm(x, g, eps=RMS_EPS):
    xf = x.astype(jnp.float32)
    y = xf * lax.rsqrt(jnp.mean(xf * xf, axis=-1, keepdims=True) + eps)
    return (y * g.astype(jnp.float32)).astype(x.dtype)


def rope(x, pos):
    dim = x.shape[-1]
    half = dim // 2
    inv_freq = ROPE_THETA ** (-jnp.arange(half, dtype=jnp.float32) * 2.0 / dim)
    ang = pos.astype(jnp.float32)[:, None] * inv_freq[None, :]
    cos = jnp.cos(ang)[None, :, None, :]
    sin = jnp.sin(ang)[None, :, None, :]
    xf = x.astype(jnp.float32)
    x1, x2 = xf[..., :half], xf[..., half:]
    return jnp.concatenate([x1 * cos - x2 * sin, x1 * sin + x2 * cos], axis=-1).astype(x.dtype)


def axial_rope(x, rows, cols):
    half = x.shape[-1] // 2
    return jnp.concatenate([rope(x[..., :half], rows), rope(x[..., half:], cols)], axis=-1)


def sweep_query_blocks(block_fn, *qs):
    b, s = qs[0].shape[:2]
    nb = s // Q_BLOCK
    blocks = tuple(jnp.moveaxis(q.reshape(b, nb, Q_BLOCK, *q.shape[2:]), 1, 0) for q in qs)
    out = lax.map(lambda qb: block_fn(*qb), blocks)
    return jnp.moveaxis(out, 0, 1).reshape(b, s, *out.shape[3:])


def gqa_axial_attention(x, w_qkv, q_g, k_g, w_o):
    b, s, _ = x.shape
    n_rows = s // GRID_W
    rows = jnp.repeat(jnp.arange(n_rows), GRID_W)
    cols = jnp.tile(jnp.arange(GRID_W), n_rows)
    q, k, v = jnp.split(x @ w_qkv, [A_HEADS * HEAD_DIM, (A_HEADS + A_KV_HEADS) * HEAD_DIM], axis=-1)
    q = axial_rope(rmsnorm(q.reshape(b, s, A_HEADS, HEAD_DIM), q_g), rows, cols)
    k = axial_rope(rmsnorm(k.reshape(b, s, A_KV_HEADS, HEAD_DIM), k_g), rows, cols)
    v = v.reshape(b, s, A_KV_HEADS, HEAD_DIM)
    q = (q * HEAD_DIM ** -0.5).reshape(b, s, A_KV_HEADS, A_GROUP, HEAD_DIM)

    def block(qb):
        sc = jnp.einsum('bqhgd,bkhd->bhgqk', qb, k, preferred_element_type=jnp.float32)
        p = jax.nn.softmax(sc, axis=-1).astype(v.dtype)
        return jnp.einsum('bhgqk,bkhd->bqhgd', p, v)

    o = sweep_query_blocks(block, q)
    return o.reshape(b, s, A_HEADS * HEAD_DIM) @ w_o


def latent_attention(x, w_down, q_lat_g, kv_lat_g, w_uq, w_ukv, q_nope_g, q_rope_g, k_nope_g, k_rope_g, w_o):
    b, s, _ = x.shape
    pos = jnp.arange(s)
    scale = (B_NOPE + B_ROPE) ** -0.5
    c_q, c_kv, k_rope = jnp.split(x @ w_down, [B_Q_LORA, B_Q_LORA + B_KV_LORA], axis=-1)
    q = (rmsnorm(c_q, q_lat_g) @ w_uq).reshape(b, s, B_HEADS, B_NOPE + B_ROPE)
    kv = (rmsnorm(c_kv, kv_lat_g) @ w_ukv).reshape(b, s, B_HEADS, B_NOPE + B_V)
    q_nope = rmsnorm(q[..., :B_NOPE], q_nope_g) * scale
    q_rope = rope(rmsnorm(q[..., B_NOPE:], q_rope_g), pos) * scale
    k_nope = rmsnorm(kv[..., :B_NOPE], k_nope_g)
    v = kv[..., B_NOPE:]
    k_rope = rope(rmsnorm(k_rope, k_rope_g)[:, :, None, :], pos)[:, :, 0, :]

    def block(qn, qr):
        sc = (jnp.einsum('bqhd,bkhd->bhqk', qn, k_nope, preferred_element_type=jnp.float32)
              + jnp.einsum('bqhr,bkr->bhqk', qr, k_rope, preferred_element_type=jnp.float32))
        p = jax.nn.softmax(sc, axis=-1).astype(v.dtype)
        return jnp.einsum('bhqk,bkhd->bqhd', p, v)

    o = sweep_query_blocks(block, q_nope, q_rope)
    return o.reshape(b, s, B_HEADS * B_V) @ w_o


def differential_attention(x, w_qkv, q_g, k_g, lam_q1, lam_k1, lam_q2, lam_k2, subln_g, w_o, layer_idx):
    b, s, _ = x.shape
    pos = jnp.arange(s)
    qk_w = 2 * C_HEADS * HEAD_DIM
    q, k, v = jnp.split(x @ w_qkv, [qk_w, 2 * qk_w], axis=-1)
    q = (rope(rmsnorm(q.reshape(b, s, 2 * C_HEADS, HEAD_DIM), q_g), pos) * HEAD_DIM ** -0.5).reshape(b, s, C_HEADS, 2, HEAD_DIM)
    k = rope(rmsnorm(k.reshape(b, s, 2 * C_HEADS, HEAD_DIM), k_g), pos).reshape(b, s, C_HEADS, 2, HEAD_DIM)
    v = v.reshape(b, s, C_HEADS, 2 * HEAD_DIM)
    k1, k2 = k[:, :, :, 0], k[:, :, :, 1]
    lam_init = LAMBDA_INIT_BASE - LAMBDA_INIT_AMP * math.exp(-LAMBDA_INIT_RATE * layer_idx)
    f32 = jnp.float32
    lam = (jnp.exp(jnp.sum(lam_q1.astype(f32) * lam_k1.astype(f32)))
           - jnp.exp(jnp.sum(lam_q2.astype(f32) * lam_k2.astype(f32))) + lam_init)

    def block(q1, q2):
        p1 = jax.nn.softmax(jnp.einsum('bqhd,bkhd->bhqk', q1, k1, preferred_element_type=f32), axis=-1)
        p2 = jax.nn.softmax(jnp.einsum('bqhd,bkhd->bhqk', q2, k2, preferred_element_type=f32), axis=-1)
        return jnp.einsum('bhqk,bkhe->bqhe', (p1 - lam * p2).astype(v.dtype), v)

    o = sweep_query_blocks(block, q[:, :, :, 0], q[:, :, :, 1])
    o = rmsnorm(o, subln_g) * (1.0 - lam_init)
    return o.reshape(b, s, C_HEADS * 2 * HEAD_DIM) @ w_o


def banded_attention(q, k, v, radius):
    n, length, h, d = q.shape
    blk = radius
    nb = -(-length // blk)
    lp = nb * blk
    qb = jnp.pad(q, ((0, 0), (0, lp - length), (0, 0), (0, 0))).reshape(n, nb, blk, h, d)

    def windows(a):
        ap = jnp.pad(a, ((0, 0), (radius, lp - length + radius), (0, 0), (0, 0))).reshape(n, nb + 2, blk, h, a.shape[-1])
        return jnp.concatenate([ap[:, :-2], ap[:, 1:-1], ap[:, 2:]], axis=2)

    kw, vw = windows(k), windows(v)
    qpos = jnp.arange(lp).reshape(nb, blk)
    kpos = (jnp.arange(nb) * blk - radius)[:, None] + jnp.arange(3 * blk)[None, :]
    valid = ((jnp.abs(qpos[:, :, None] - kpos[:, None, :]) <= radius)
             & (kpos[:, None, :] >= 0) & (kpos[:, None, :] < length))
    sc = jnp.einsum('nbqhd,nbkhd->nbhqk', qb, kw, preferred_element_type=jnp.float32)
    sc = jnp.where(valid[None, :, None], sc, -jnp.inf)
    m = jnp.max(sc, axis=-1, keepdims=True)
    e = jnp.exp(sc - m)
    den = jnp.sum(e, axis=-1, keepdims=True)
    o = jnp.einsum('nbhqk,nbkhd->nbqhd', (e / den).astype(v.dtype), vw)
    lse = (m + jnp.log(den))[..., 0]
    o = o.reshape(n, lp, h, d)[:, :length]
    lse = jnp.swapaxes(lse, 2, 3).reshape(n, lp, h)[:, :length]
    return o, lse


def _to_strided(a, dil):
    b, s = a.shape[:2]
    return jnp.swapaxes(a.reshape(b, s // dil, dil, *a.shape[2:]), 1, 2).reshape(b * dil, s // dil, *a.shape[2:])


def _from_strided(a, b, dil):
    length = a.shape[1]
    return jnp.swapaxes(a.reshape(b, dil, length, *a.shape[2:]), 1, 2).reshape(b, dil * length, *a.shape[2:])


def dilated_attention(x, w_qkv, q_g, k_g, w_o):
    b, s, _ = x.shape
    pos = jnp.arange(s)
    q, k, v = (t.reshape(b, s, D_HEADS, HEAD_DIM) for t in jnp.split(x @ w_qkv, 3, axis=-1))
    q = rope(rmsnorm(q, q_g), pos) * HEAD_DIM ** -0.5
    k = rope(rmsnorm(k, k_g), pos)
    outs, lses = [], []
    for window, dil in DILATED_BRANCHES:
        o, lse = banded_attention(_to_strided(q, dil), _to_strided(k, dil), _to_strided(v, dil), window // (2 * dil))
        outs.append(_from_strided(o, b, dil))
        lses.append(_from_strided(lse, b, dil))
    wts = jax.nn.softmax(jnp.stack(lses), axis=0)
    o = jnp.einsum('gbsh,gbshd->bshd', wts, jnp.stack(outs).astype(jnp.float32)).astype(x.dtype)
    return o.reshape(b, s, D_HEADS * HEAD_DIM) @ w_o


def expert_choice_ffn(x, w_router, w_gate, w_up, w_down):
    b, s, d = x.shape
    n = b * s
    xf = x.reshape(n, d)
    aff = jax.nn.softmax((xf @ w_router).astype(jnp.float32), axis=-1)
    cap = (EC_CAPACITY * n) // N_EXPERTS
    g, idx = lax.top_k(aff.T, cap)
    xg = xf[idx]
    h = jax.nn.silu(jnp.einsum('ecd,edf->ecf', xg, w_gate)) * jnp.einsum('ecd,edf->ecf', xg, w_up)
    y = jnp.einsum('ecf,efd->ecd', h, w_down) * g[..., None].astype(x.dtype)
    out = jnp.zeros_like(xf).at[idx.reshape(-1)].add(y.reshape(-1, d))
    return out.reshape(b, s, d)


def setup_inputs(seed: int = 0) -> dict:
    key = jax.random.key(seed)
    keys = list(jax.random.split(key, 40))

    def nxt():
        return keys.pop()

    def dense(shape, fan_in):
        return jax.random.normal(nxt(), shape, jnp.float32) * fan_in ** -0.5

    def gain(shape):
        return 1.0 + 0.02 * jax.random.normal(nxt(), shape, jnp.float32)

    def small(shape, std):
        return std * jax.random.normal(nxt(), shape, jnp.float32)

    n_a, n_b, n_c, n_d = (len(range(m, DEPTH, N_MIXERS)) for m in range(N_MIXERS))
    d = D_MODEL
    return {
        'x_prompt': jax.random.normal(nxt(), (BATCH, SEQ, d), jnp.float32),
        'x_sample': jax.random.normal(nxt(), (DEC_BATCH, DEC_SEQ, d), jnp.float32),
        'norm_mix_g': gain((DEPTH, d)),
        'norm_ffn_g': gain((DEPTH, d)),
        'a_w_qkv': dense((n_a, d, (A_HEADS + 2 * A_KV_HEADS) * HEAD_DIM), d),
        'a_q_norm_g': gain((n_a, HEAD_DIM)),
        'a_k_norm_g': gain((n_a, HEAD_DIM)),
        'a_w_o': dense((n_a, A_HEADS * HEAD_DIM, d), A_HEADS * HEAD_DIM),
        'b_w_down': dense((n_b, d, B_Q_LORA + B_KV_LORA + B_ROPE), d),
        'b_q_lat_norm_g': gain((n_b, B_Q_LORA)),
        'b_kv_lat_norm_g': gain((n_b, B_KV_LORA)),
        'b_w_uq': dense((n_b, B_Q_LORA, B_HEADS * (B_NOPE + B_ROPE)), B_Q_LORA),
        'b_w_ukv': dense((n_b, B_KV_LORA, B_HEADS * (B_NOPE + B_V)), B_KV_LORA),
        'b_q_nope_norm_g': gain((n_b, B_NOPE)),
        'b_q_rope_norm_g': gain((n_b, B_ROPE)),
        'b_k_nope_norm_g': gain((n_b, B_NOPE)),
        'b_k_rope_norm_g': gain((n_b, B_ROPE)),
        'b_w_o': dense((n_b, B_HEADS * B_V, d), B_HEADS * B_V),
        'c_w_qkv': dense((n_c, d, 3 * 2 * C_HEADS * HEAD_DIM), d),
        'c_q_norm_g': gain((n_c, HEAD_DIM)),
        'c_k_norm_g': gain((n_c, HEAD_DIM)),
        'c_lambda_q1': small((n_c, HEAD_DIM), LAMBDA_STD),
        'c_lambda_k1': small((n_c, HEAD_DIM), LAMBDA_STD),
        'c_lambda_q2': small((n_c, HEAD_DIM), LAMBDA_STD),
        'c_lambda_k2': small((n_c, HEAD_DIM), LAMBDA_STD),
        'c_subln_g': gain((n_c, 2 * HEAD_DIM)),
        'c_w_o': dense((n_c, 2 * C_HEADS * HEAD_DIM, d), 2 * C_HEADS * HEAD_DIM),
        'd_w_qkv': dense((n_d, d, 3 * D_HEADS * HEAD_DIM), d),
        'd_q_norm_g': gain((n_d, HEAD_DIM)),
        'd_k_norm_g': gain((n_d, HEAD_DIM)),
        'd_w_o': dense((n_d, D_HEADS * HEAD_DIM, d), D_HEADS * HEAD_DIM),
        'ec_w_router': dense((DEPTH, d, N_EXPERTS), d),
        'ec_w_gate': dense((DEPTH, N_EXPERTS, d, EXPERT_FF), d),
        'ec_w_up': dense((DEPTH, N_EXPERTS, d, EXPERT_FF), d),
        'ec_w_down': dense((DEPTH, N_EXPERTS, EXPERT_FF, d), EXPERT_FF),
    }


def reference(x_prompt, x_sample, norm_mix_g, norm_ffn_g,
              a_w_qkv, a_q_norm_g, a_k_norm_g, a_w_o,
              b_w_down, b_q_lat_norm_g, b_kv_lat_norm_g, b_w_uq, b_w_ukv,
              b_q_nope_norm_g, b_q_rope_norm_g, b_k_nope_norm_g, b_k_rope_norm_g, b_w_o,
              c_w_qkv, c_q_norm_g, c_k_norm_g, c_lambda_q1, c_lambda_k1, c_lambda_q2, c_lambda_k2,
              c_subln_g, c_w_o,
              d_w_qkv, d_q_norm_g, d_k_norm_g, d_w_o,
              ec_w_router, ec_w_gate, ec_w_up, ec_w_down):

    def trunk(x):
        for i in range(DEPTH):
            m, j = i % N_MIXERS, i // N_MIXERS
            h = rmsnorm(x, norm_mix_g[i])
            if m == 0:
                h = gqa_axial_attention(h, a_w_qkv[j], a_q_norm_g[j], a_k_norm_g[j], a_w_o[j])
            elif m == 1:
                h = latent_attention(h, b_w_down[j], b_q_lat_norm_g[j], b_kv_lat_norm_g[j], b_w_uq[j], b_w_ukv[j],
                                     b_q_nope_norm_g[j], b_q_rope_norm_g[j], b_k_nope_norm_g[j], b_k_rope_norm_g[j], b_w_o[j])
            elif m == 2:
                h = differential_attention(h, c_w_qkv[j], c_q_norm_g[j], c_k_norm_g[j], c_lambda_q1[j], c_lambda_k1[j],
                                           c_lambda_q2[j], c_lambda_k2[j], c_subln_g[j], c_w_o[j], i)
            else:
                h = dilated_attention(h, d_w_qkv[j], d_q_norm_g[j], d_k_norm_g[j], d_w_o[j])
            x = x + h
            x = x + expert_choice_ffn(rmsnorm(x, norm_ffn_g[i]), ec_w_router[i], ec_w_gate[i], ec_w_up[i], ec_w_down[i])
        return x

    y_prompt = trunk(x_prompt)
    y_sample = trunk(x_sample)
    return (y_prompt, y_sample)
```

```python
import functools
import math

import jax
import jax.numpy as jnp
from jax import lax
from jax.experimental import pallas as pl
from jax.experimental.pallas import tpu as pltpu

F32 = jnp.float32
BF16 = jnp.bfloat16

D_MODEL = 1024
HEAD_DIM = 64
GRID_W = 64
ROPE_THETA = 10000.0
RMS_EPS = 1e-6
N_MIXERS = 4
A_HEADS, A_KV_HEADS = 16, 4
B_HEADS, B_Q_LORA, B_KV_LORA, B_NOPE, B_ROPE, B_V = 16, 384, 256, 64, 32, 64
C_HEADS = 8
LAMBDA_INIT_BASE, LAMBDA_INIT_AMP, LAMBDA_INIT_RATE = 0.8, 0.6, 0.3
D_HEADS = 16
DILATED_BRANCHES = ((128, 1), (512, 4), (2048, 16))
N_EXPERTS = 16
EC_CAPACITY = 2
EXPERT_FF = 2 * D_MODEL

LANES = 128
MXU_DIM = 256
VMEM_LIMIT = 56 * 1024 * 1024
NEG_BIG = -1e30


def _cparams(sem):
    return pltpu.CompilerParams(dimension_semantics=sem, vmem_limit_bytes=VMEM_LIMIT)


def _rms_rows(x, g):
    return x * lax.rsqrt(jnp.mean(x * x, axis=-1, keepdims=True) + RMS_EPS) * g


def _head_rms(t, bd, gain, hd):
    outs = []
    for j in range(t.shape[1] // MXU_DIM):
        tj = t[:, j * MXU_DIM:(j + 1) * MXU_DIM]
        ss = jnp.dot((tj * tj).astype(BF16), bd, preferred_element_type=F32)
        outs.append(tj * lax.rsqrt(ss * (1.0 / hd) + RMS_EPS))
    y = outs[0] if len(outs) == 1 else jnp.concatenate(outs, axis=-1)
    return y * gain


def _rope_slabs(y, cos, sin, half):
    lane = lax.broadcasted_iota(jnp.int32, (1, LANES), 1)
    first = (lane % (2 * half)) < half
    outs = []
    for j in range(y.shape[1] // LANES):
        yj = y[:, j * LANES:(j + 1) * LANES]
        swapped = jnp.where(first, pltpu.roll(yj, LANES - half, 1), pltpu.roll(yj, half, 1))
        outs.append(yj * cos + swapped * sin)
    return outs[0] if len(outs) == 1 else jnp.concatenate(outs, axis=-1)


def _qkv_kernel(x_ref, g_ref, w_ref, bd_ref, gain_ref, cos_ref, sin_ref, q_ref, k_ref, v_ref,
                *, nq, nk, half, q_scale):
    xn = _rms_rows(x_ref[...], g_ref[...]).astype(BF16)
    y = jnp.dot(xn, w_ref[...], preferred_element_type=F32)
    bd = bd_ref[...]
    cos, sin = cos_ref[...], sin_ref[...]
    gain = gain_ref[...]
    q = _rope_slabs(_head_rms(y[:, :nq], bd, gain[:, :nq], HEAD_DIM), cos, sin, half) * q_scale
    k = _rope_slabs(_head_rms(y[:, nq:nq + nk], bd, gain[:, nq:], HEAD_DIM), cos, sin, half)
    q_ref[...] = q.astype(BF16)
    k_ref[...] = k.astype(BF16)
    v_ref[...] = y[:, nq + nk:].astype(BF16)


def _qkv_proj(x, g, w, gain, cos, sin, bd, *, nq, nk, nv, half, q_scale, seq, tm):
    n, d = x.shape
    nt = seq // tm
    row = lambda i: (i, 0)
    const = lambda i: (0, 0)
    return pl.pallas_call(
        functools.partial(_qkv_kernel, nq=nq, nk=nk, half=half, q_scale=q_scale),
        grid=(n // tm,),
        in_specs=[pl.BlockSpec((tm, d), row), pl.BlockSpec((1, d), const),
                  pl.BlockSpec(w.shape, const), pl.BlockSpec(bd.shape, const),
                  pl.BlockSpec(gain.shape, const),
                  pl.BlockSpec((tm, LANES), lambda i: (i % nt, 0)),
                  pl.BlockSpec((tm, LANES), lambda i: (i % nt, 0))],
        out_specs=[pl.BlockSpec((tm, nq), row), pl.BlockSpec((tm, nk), row), pl.BlockSpec((tm, nv), row)],
        out_shape=[jax.ShapeDtypeStruct((n, nq), BF16), jax.ShapeDtypeStruct((n, nk), BF16),
                   jax.ShapeDtypeStruct((n, nv), BF16)],
        compiler_params=_cparams(("parallel",)),
        name="qkv_proj",
    )(x, g, w, bd, gain, cos, sin)


def _mla_kernel(x_ref, g_ref, wd_ref, qlg_ref, kvlg_ref, wuq_ref, wukv_ref, bd64_ref, bd32_ref,
                gq_ref, gk_ref, cos_ref, sin_ref, q_ref, k_ref, v_ref, *, scale):
    xn = _rms_rows(x_ref[...], g_ref[...]).astype(BF16)
    c = jnp.dot(xn, wd_ref[...], preferred_element_type=F32)
    cq = _rms_rows(c[:, :B_Q_LORA], qlg_ref[...]).astype(BF16)
    ckv = _rms_rows(c[:, B_Q_LORA:B_Q_LORA + B_KV_LORA], kvlg_ref[...]).astype(BF16)
    q = jnp.dot(cq, wuq_ref[...], preferred_element_type=F32)
    kv = jnp.dot(ckv, wukv_ref[...], preferred_element_type=F32)
    bd64, bd32 = bd64_ref[...], bd32_ref[...]
    cos, sin = cos_ref[...], sin_ref[...]
    gq, gk = gq_ref[...], gk_ref[...]
    hw = B_HEADS * B_NOPE
    qn = _head_rms(q[:, :hw], bd64, gq[:, :hw], B_NOPE) * scale
    qr = _rope_slabs(_head_rms(q[:, hw:], bd32, gq[:, hw:], B_ROPE), cos, sin, B_ROPE // 2) * scale
    kn = _head_rms(kv[:, :hw], bd64, gk[:, :hw], B_NOPE)
    kr_raw = c[:, B_Q_LORA + B_KV_LORA:]
    kr2 = _head_rms(jnp.concatenate([kr_raw, kr_raw], axis=-1), bd32,
                    jnp.concatenate([gk[:, hw:], gk[:, hw:]], axis=-1), B_ROPE)
    kr = _rope_slabs(kr2[:, :LANES], cos, sin, B_ROPE // 2).astype(BF16)
    qn, qr, kn = qn.astype(BF16), qr.astype(BF16), kn.astype(BF16)
    for p in range(B_HEADS // 2):
        sl = slice(p * LANES, (p + 1) * LANES)
        q_ref[:, 2 * p * LANES:(2 * p + 1) * LANES] = qn[:, sl]
        q_ref[:, (2 * p + 1) * LANES:(2 * p + 2) * LANES] = qr[:, sl]
        k_ref[:, 2 * p * LANES:(2 * p + 1) * LANES] = kn[:, sl]
        k_ref[:, (2 * p + 1) * LANES:(2 * p + 2) * LANES] = kr
    v_ref[...] = kv[:, hw:].astype(BF16)


def _mla_proj(x, g, wd, qlg, kvlg, wuq, wukv, bd64, bd32, gq, gk, cos, sin, *, seq, tm):
    n, d = x.shape
    nt = seq // tm
    row = lambda i: (i, 0)
    const = lambda i: (0, 0)
    full = lambda a: pl.BlockSpec(a.shape, const)
    wq = (B_HEADS // 2) * 2 * LANES
    return pl.pallas_call(
        functools.partial(_mla_kernel, scale=(B_NOPE + B_ROPE) ** -0.5),
        grid=(n // tm,),
        in_specs=[pl.BlockSpec((tm, d), row), full(g), full(wd), full(qlg), full(kvlg), full(wuq), full(wukv),
                  full(bd64), full(bd32), full(gq), full(gk),
                  pl.BlockSpec((tm, LANES), lambda i: (i % nt, 0)),
                  pl.BlockSpec((tm, LANES), lambda i: (i % nt, 0))],
        out_specs=[pl.BlockSpec((tm, wq), row), pl.BlockSpec((tm, wq), row),
                   pl.BlockSpec((tm, B_HEADS * B_V), row)],
        out_shape=[jax.ShapeDtypeStruct((n, wq), BF16), jax.ShapeDtypeStruct((n, wq), BF16),
                   jax.ShapeDtypeStruct((n, B_HEADS * B_V), BF16)],
        compiler_params=_cparams(("parallel",)),
        name="mla_proj",
    )(x, g, wd, qlg, kvlg, wuq, wukv, bd64, bd32, gq, gk, cos, sin)


def _attn_kernel(*refs, mode, qw, tq, tk, nk, band):
    it = iter(refs)
    q_ref, k_ref, v_ref = next(it), next(it), next(it)
    bias_ref = next(it) if band is not None else None
    if mode == "diff":
        lam_ref, subg_ref = next(it), next(it)
    o_ref = next(it)
    m_sc, l_sc, acc_sc = next(it), next(it), next(it)

    qi = pl.program_id(2)
    lane = lax.broadcasted_iota(jnp.int32, (1, qw), 1)
    if qw == LANES:
        mask_a = lane < HEAD_DIM
        mask_b = lane >= HEAD_DIM
    else:
        mask_a = (lane < HEAD_DIM) | ((lane >= LANES) & (lane < LANES + B_ROPE))
        mask_b = ((lane >= HEAD_DIM) & (lane < LANES)) | ((lane >= LANES + B_ROPE) & (lane < LANES + 2 * B_ROPE))
    q = q_ref[0]
    zero = jnp.zeros_like(q)
    qs = (jnp.where(mask_a, q, zero), jnp.where(mask_b, q, zero))

    m_sc[...] = jnp.full(m_sc.shape, NEG_BIG, F32)
    l_sc[...] = jnp.zeros(l_sc.shape, F32)
    acc_sc[...] = jnp.zeros(acc_sc.shape, F32)

    def body(c, carry):
        start = pl.multiple_of(c * tk, tk)
        k = k_ref[0, pl.ds(start, tk), :]
        v = v_ref[0, pl.ds(start, tk), :]
        bias = bias_ref[c - qi + band] if band is not None else None
        for h in range(2):
            s = lax.dot_general(qs[h], k, (((1,), (1,)), ((), ())), preferred_element_type=F32)
            if bias is not None:
                s = s + bias
            m_old = m_sc[h]
            m_new = jnp.maximum(m_old, jnp.max(s, axis=-1, keepdims=True))
            p = jnp.exp(s - m_new)
            alpha = jnp.exp(m_old - m_new)
            l_sc[h] = alpha * l_sc[h] + jnp.sum(p, axis=-1, keepdims=True)
            acc_sc[h] = alpha * acc_sc[h] + jnp.dot(p.astype(BF16), v, preferred_element_type=F32)
            m_sc[h] = m_new
        return carry

    if band is None:
        lax.fori_loop(0, nk, body, 0)
    else:
        lax.fori_loop(jnp.maximum(qi - band, 0), jnp.minimum(qi + band + 1, nk), body, 0)

    oa = acc_sc[0] / l_sc[0]
    ob = acc_sc[1] / l_sc[1]
    if mode == "pair":
        olane = lax.broadcasted_iota(jnp.int32, (1, LANES), 1)
        o_ref[0] = jnp.where(olane < HEAD_DIM, oa, ob).astype(o_ref.dtype)
    else:
        lam = lam_ref[0, 0]
        lam_init = lam_ref[0, 1]
        o = oa - lam * ob
        o = _rms_rows(o, subg_ref[...]) * (1.0 - lam_init)
        o_ref[0] = o.astype(o_ref.dtype)


def _attention(q, k, v, *, mode, qw, kmap, tq, tk, bias=None, band=None, lam=None, subg=None):
    b, s, cq = q.shape
    pairs = cq // qw
    nk = s // tk
    in_specs = [pl.BlockSpec((1, tq, qw), lambda bi, p, i: (bi, i, p)),
                pl.BlockSpec((1, s, qw), lambda bi, p, i: (bi, 0, kmap(p))),
                pl.BlockSpec((1, s, LANES), lambda bi, p, i: (bi, 0, kmap(p)))]
    args = [q, k, v]
    if band is not None:
        in_specs.append(pl.BlockSpec(bias.shape, lambda bi, p, i: (0, 0, 0)))
        args.append(bias)
    if mode == "diff":
        in_specs.append(pl.BlockSpec(memory_space=pltpu.SMEM))
        in_specs.append(pl.BlockSpec(subg.shape, lambda bi, p, i: (0, 0)))
        args += [lam, subg]
    return pl.pallas_call(
        functools.partial(_attn_kernel, mode=mode, qw=qw, tq=tq, tk=tk, nk=nk, band=band),
        grid=(b, pairs, s // tq),
        in_specs=in_specs,
        out_specs=pl.BlockSpec((1, tq, LANES), lambda bi, p, i: (bi, i, p)),
        out_shape=jax.ShapeDtypeStruct((b, s, pairs * LANES), BF16),
        scratch_shapes=[pltpu.VMEM((2, tq, 1), F32), pltpu.VMEM((2, tq, 1), F32),
                        pltpu.VMEM((2, tq, LANES), F32)],
        compiler_params=_cparams(("parallel", "parallel", "arbitrary")),
        name="attention_" + mode,
    )(*args)


def _oproj_kernel(o_ref, w_ref, x_ref, g_ref, wr_ref, xnew_ref, xn_ref, aff_ref):
    xnew = x_ref[...] + jnp.dot(o_ref[...], w_ref[...], preferred_element_type=F32)
    xnew_ref[...] = xnew
    xn = _rms_rows(xnew, g_ref[...]).astype(BF16)
    xn_ref[...] = xn
    logits = lax.dot_general(wr_ref[...], xn, (((1,), (1,)), ((), ())), preferred_element_type=F32)
    z = logits - jnp.max(logits, axis=0, keepdims=True)
    e = jnp.exp(z)
    aff_ref[...] = e / jnp.sum(e, axis=0, keepdims=True)


def _oproj(o, w, x, g, wr_t, *, tm):
    n, d = x.shape
    row = lambda i: (i, 0)
    const = lambda i: (0, 0)
    return pl.pallas_call(
        _oproj_kernel,
        grid=(n // tm,),
        in_specs=[pl.BlockSpec((tm, o.shape[1]), row), pl.BlockSpec(w.shape, const), pl.BlockSpec((tm, d), row),
                  pl.BlockSpec((1, d), const), pl.BlockSpec(wr_t.shape, const)],
        out_specs=[pl.BlockSpec((tm, d), row), pl.BlockSpec((tm, d), row),
                   pl.BlockSpec((N_EXPERTS, tm), lambda i: (0, i))],
        out_shape=[jax.ShapeDtypeStruct((n, d), F32), jax.ShapeDtypeStruct((n, d), BF16),
                   jax.ShapeDtypeStruct((N_EXPERTS, n), F32)],
        compiler_params=_cparams(("parallel",)),
        name="oproj_router",
    )(o, w, x, g, wr_t)


def _ffn_kernel(xg_ref, wg_ref, wu_ref, wd_ref, gate_ref, y_ref):
    xg = xg_ref[0]
    a = jnp.dot(xg, wg_ref[0], preferred_element_type=F32)
    u = jnp.dot(xg, wu_ref[0], preferred_element_type=F32)
    h = (a * jax.nn.sigmoid(a) * u).astype(BF16)
    y_ref[0] = jnp.dot(h, wd_ref[0], preferred_element_type=F32) * gate_ref[0]


def _expert_ffn(xg, wg, wu, wd, gate, *, tm):
    e, cap, d = xg.shape
    ff = wg.shape[2]
    return pl.pallas_call(
        _ffn_kernel,
        grid=(e, cap // tm),
        in_specs=[pl.BlockSpec((1, tm, d), lambda ei, i: (ei, i, 0)),
                  pl.BlockSpec((1, d, ff), lambda ei, i: (ei, 0, 0)),
                  pl.BlockSpec((1, d, ff), lambda ei, i: (ei, 0, 0)),
                  pl.BlockSpec((1, ff, d), lambda ei, i: (ei, 0, 0)),
                  pl.BlockSpec((1, tm, 1), lambda ei, i: (ei, i, 0))],
        out_specs=pl.BlockSpec((1, tm, d), lambda ei, i: (ei, i, 0)),
        out_shape=jax.ShapeDtypeStruct((e, cap, d), F32),
        compiler_params=_cparams(("parallel", "arbitrary")),
        name="expert_ffn",
    )(xg, wg, wu, wd, gate)


def _block_diag(hd):
    i = jnp.arange(MXU_DIM)
    return (i[:, None] // hd == i[None, :] // hd).astype(BF16)


def _rope_tables_std(seq, dim):
    half = dim // 2
    inv_freq = ROPE_THETA ** (-jnp.arange(half, dtype=F32) * 2.0 / dim)
    ang = jnp.arange(seq).astype(F32)[:, None] * inv_freq[None, :]
    cos, sin = jnp.cos(ang), jnp.sin(ang)
    reps = LANES // dim
    return (jnp.tile(jnp.concatenate([cos, cos], axis=-1), (1, reps)),
            jnp.tile(jnp.concatenate([-sin, sin], axis=-1), (1, reps)))


def _rope_tables_axial(seq):
    sub = HEAD_DIM // 2
    half = sub // 2
    inv_freq = ROPE_THETA ** (-jnp.arange(half, dtype=F32) * 2.0 / sub)
    n_rows = seq // GRID_W
    rows = jnp.repeat(jnp.arange(n_rows), GRID_W).astype(F32)
    cols = jnp.tile(jnp.arange(GRID_W), n_rows).astype(F32)
    ar, ac = rows[:, None] * inv_freq[None, :], cols[:, None] * inv_freq[None, :]
    cos = jnp.concatenate([jnp.cos(ar), jnp.cos(ar), jnp.cos(ac), jnp.cos(ac)], axis=-1)
    sin = jnp.concatenate([-jnp.sin(ar), jnp.sin(ar), -jnp.sin(ac), jnp.sin(ac)], axis=-1)
    return jnp.tile(cos, (1, 2)), jnp.tile(sin, (1, 2))


def _dilated_bias(t, band):
    rel = jnp.arange(-band, band + 1)[:, None, None] * t
    d = rel + jnp.arange(t)[None, None, :] - jnp.arange(t)[None, :, None]
    cnt = jnp.zeros(d.shape, F32)
    for window, dil in DILATED_BRANCHES:
        cnt = cnt + ((d % dil == 0) & (jnp.abs(d) <= window // 2)).astype(F32)
    return jnp.where(cnt > 0, jnp.log(jnp.maximum(cnt, 1.0)), NEG_BIG)


def _tile_gain(g, reps):
    return jnp.tile(g.astype(F32), reps)[None, :]


def _pick(n, pref):
    t = min(n, pref)
    while n % t:
        t //= 2
    return t


def _mixer_a(x, b, s, g, w_qkv, q_g, k_g):
    nq = A_HEADS * HEAD_DIM
    wq, wk, wv = w_qkv[:, :nq], w_qkv[:, nq:nq + A_KV_HEADS * HEAD_DIM], w_qkv[:, nq + A_KV_HEADS * HEAD_DIM:]
    dup = lambda w: jnp.repeat(w.reshape(D_MODEL, A_KV_HEADS, 1, HEAD_DIM), 2, axis=2).reshape(D_MODEL, -1)
    w = jnp.concatenate([wq, dup(wk), dup(wv)], axis=1).astype(BF16)
    nk = 2 * A_KV_HEADS * HEAD_DIM
    gain = jnp.concatenate([_tile_gain(q_g, nq // HEAD_DIM), _tile_gain(k_g, nk // HEAD_DIM)], axis=1)
    cos, sin = _rope_tables_axial(s)
    tm = _pick(s, 512)
    q, k, v = _qkv_proj(x, g, w, gain, cos, sin, _block_diag(HEAD_DIM), nq=nq, nk=nk, nv=nk,
                        half=HEAD_DIM // 4, q_scale=HEAD_DIM ** -0.5, seq=s, tm=tm)
    group_pairs = (A_HEADS // A_KV_HEADS) // 2
    t = _pick(s, 512)
    return _attention(q.reshape(b, s, -1), k.reshape(b, s, -1), v.reshape(b, s, -1), mode="pair", qw=LANES,
                      kmap=lambda p: p // group_pairs, tq=t, tk=t)


def _mixer_b(x, b, s, g, w_down, q_lat_g, kv_lat_g, w_uq, w_ukv, q_nope_g, q_rope_g, k_nope_g, k_rope_g):
    zpad = lambda a, n: jnp.concatenate([a, jnp.zeros(a.shape[:-1] + (n,), a.dtype)], axis=-1)
    w_kr = w_down[:, B_Q_LORA + B_KV_LORA:]
    wd = jnp.concatenate([w_down[:, :B_Q_LORA + B_KV_LORA], zpad(jnp.concatenate([w_kr, w_kr], axis=1), 2 * B_ROPE)],
                         axis=1).astype(BF16)
    uq = w_uq.reshape(B_Q_LORA, B_HEADS, B_NOPE + B_ROPE)
    uq_n = uq[:, :, :B_NOPE].reshape(B_Q_LORA, -1)
    uq_r = zpad(uq[:, :, B_NOPE:].reshape(B_Q_LORA, B_HEADS // 2, 2 * B_ROPE), LANES - 2 * B_ROPE).reshape(B_Q_LORA, -1)
    wuq = jnp.concatenate([uq_n, uq_r], axis=1).astype(BF16)
    ukv = w_ukv.reshape(B_KV_LORA, B_HEADS, B_NOPE + B_V)
    wukv = jnp.concatenate([ukv[:, :, :B_NOPE].reshape(B_KV_LORA, -1), ukv[:, :, B_NOPE:].reshape(B_KV_LORA, -1)],
                           axis=1).astype(BF16)
    gq = jnp.concatenate([_tile_gain(q_nope_g, B_HEADS), _tile_gain(q_rope_g, B_HEADS * B_NOPE // B_ROPE)], axis=1)
    gk = jnp.concatenate([_tile_gain(k_nope_g, B_HEADS), _tile_gain(k_rope_g, LANES // B_ROPE)], axis=1)
    cos, sin = _rope_tables_std(s, B_ROPE)
    tm = _pick(s, 512)
    q, k, v = _mla_proj(x, g, wd, q_lat_g.astype(F32)[None, :], kv_lat_g.astype(F32)[None, :], wuq, wukv,
                        _block_diag(B_NOPE), _block_diag(B_ROPE), gq, gk, cos, sin, seq=s, tm=tm)
    t = _pick(s, 512)
    return _attention(q.reshape(b, s, -1), k.reshape(b, s, -1), v.reshape(b, s, -1), mode="pair", qw=2 * LANES,
                      kmap=lambda p: p, tq=t, tk=t)


def _mixer_c(x, b, s, g, w_qkv, q_g, k_g, lq1, lk1, lq2, lk2, subln_g, layer_idx):
    nq = 2 * C_HEADS * HEAD_DIM
    gain = jnp.concatenate([_tile_gain(q_g, nq // HEAD_DIM), _tile_gain(k_g, nq // HEAD_DIM)], axis=1)
    cos, sin = _rope_tables_std(s, HEAD_DIM)
    tm = _pick(s, 512)
    q, k, v = _qkv_proj(x, g, w_qkv.astype(BF16), gain, cos, sin, _block_diag(HEAD_DIM), nq=nq, nk=nq,
                        nv=C_HEADS * 2 * HEAD_DIM, half=HEAD_DIM // 2, q_scale=HEAD_DIM ** -0.5, seq=s, tm=tm)
    lam_init = LAMBDA_INIT_BASE - LAMBDA_INIT_AMP * math.exp(-LAMBDA_INIT_RATE * layer_idx)
    lam = (jnp.exp(jnp.sum(lq1.astype(F32) * lk1.astype(F32))) - jnp.exp(jnp.sum(lq2.astype(F32) * lk2.astype(F32)))
           + lam_init)
    lam_arr = jnp.stack([lam, jnp.asarray(lam_init, F32)]).reshape(1, 2).astype(F32)
    t = _pick(s, 512)
    return _attention(q.reshape(b, s, -1), k.reshape(b, s, -1), v.reshape(b, s, -1), mode="diff", qw=LANES,
                      kmap=lambda p: p, tq=t, tk=t, lam=lam_arr, subg=subln_g.astype(F32)[None, :])


def _mixer_d(x, b, s, g, w_qkv, q_g, k_g):
    nq = D_HEADS * HEAD_DIM
    gain = jnp.concatenate([_tile_gain(q_g, D_HEADS), _tile_gain(k_g, D_HEADS)], axis=1)
    cos, sin = _rope_tables_std(s, HEAD_DIM)
    tm = _pick(s, 512)
    q, k, v = _qkv_proj(x, g, w_qkv.astype(BF16), gain, cos, sin, _block_diag(HEAD_DIM), nq=nq, nk=nq, nv=nq,
                        half=HEAD_DIM // 2, q_scale=HEAD_DIM ** -0.5, seq=s, tm=tm)
    t = _pick(s, 512)
    reach = max(w // 2 for w, _ in DILATED_BRANCHES)
    band = -(-reach // t)
    return _attention(q.reshape(b, s, -1), k.reshape(b, s, -1), v.reshape(b, s, -1), mode="pair", qw=LANES,
                      kmap=lambda p: p, tq=t, tk=t, bias=_dilated_bias(t, band), band=band)


def _ec_ffn(xnew, xn, aff, wg, wu, wd):
    n, d = xnew.shape
    cap = (EC_CAPACITY * n) // N_EXPERTS
    gate, idx = lax.top_k(aff, cap)
    xg = xn[idx]
    y = _expert_ffn(xg, wg, wu, wd, gate[..., None], tm=_pick(cap, 512))
    return xnew.at[idx.reshape(-1)].add(y.reshape(-1, d))


def kernel(x_prompt, x_sample, norm_mix_g, norm_ffn_g, a_w_qkv, a_q_norm_g, a_k_norm_g, a_w_o, b_w_down, b_q_lat_norm_g, b_kv_lat_norm_g, b_w_uq, b_w_ukv, b_q_nope_norm_g, b_q_rope_norm_g, b_k_nope_norm_g, b_k_rope_norm_g, b_w_o, c_w_qkv, c_q_norm_g, c_k_norm_g, c_lambda_q1, c_lambda_k1, c_lambda_q2, c_lambda_k2, c_subln_g, c_w_o, d_w_qkv, d_q_norm_g, d_k_norm_g, d_w_o, ec_w_router, ec_w_gate, ec_w_up, ec_w_down):
    depth = norm_mix_g.shape[0]
    wg_all, wu_all, wd_all = ec_w_gate.astype(BF16), ec_w_up.astype(BF16), ec_w_down.astype(BF16)

    def trunk(x3):
        b, s, d = x3.shape
        x = x3.reshape(b * s, d)
        for i in range(depth):
            m, j = i % N_MIXERS, i // N_MIXERS
            g = norm_mix_g[i].astype(F32)[None, :]
            if m == 0:
                o, w_o = _mixer_a(x, b, s, g, a_w_qkv[j], a_q_norm_g[j], a_k_norm_g[j]), a_w_o[j]
            elif m == 1:
                o = _mixer_b(x, b, s, g, b_w_down[j], b_q_lat_norm_g[j], b_kv_lat_norm_g[j], b_w_uq[j], b_w_ukv[j],
                             b_q_nope_norm_g[j], b_q_rope_norm_g[j], b_k_nope_norm_g[j], b_k_rope_norm_g[j])
                w_o = b_w_o[j]
            elif m == 2:
                o = _mixer_c(x, b, s, g, c_w_qkv[j], c_q_norm_g[j], c_k_norm_g[j], c_lambda_q1[j], c_lambda_k1[j],
                             c_lambda_q2[j], c_lambda_k2[j], c_subln_g[j], i)
                w_o = c_w_o[j]
            else:
                o, w_o = _mixer_d(x, b, s, g, d_w_qkv[j], d_q_norm_g[j], d_k_norm_g[j]), d_w_o[j]
            xnew, xn, aff = _oproj(o.reshape(b * s, -1), w_o.astype(BF16), x, norm_ffn_g[i].astype(F32)[None, :],
                                   ec_w_router[i].T.astype(BF16), tm=_pick(b * s, 512))
            x = _ec_ffn(xnew, xn, aff, wg_all[i], wu_all[i], wd_all[i])
        return x.reshape(b, s, d)

    return (trunk(x_prompt), trunk(x_sample))
```

```python
import functools
import math

import jax
import jax.numpy as jnp
from jax import lax
from jax.experimental import pallas as pl
from jax.experimental.pallas import tpu as pltpu

F32 = jnp.float32
BF16 = jnp.bfloat16

D_MODEL = 1024
HEAD_DIM = 64
GRID_W = 64
ROPE_THETA = 10000.0
RMS_EPS = 1e-6
N_MIXERS = 4
A_HEADS, A_KV_HEADS = 16, 4
B_HEADS, B_Q_LORA, B_KV_LORA, B_NOPE, B_ROPE, B_V = 16, 384, 256, 64, 32, 64
C_HEADS = 8
LAMBDA_INIT_BASE, LAMBDA_INIT_AMP, LAMBDA_INIT_RATE = 0.8, 0.6, 0.3
D_HEADS = 16
DILATED_BRANCHES = ((128, 1), (512, 4), (2048, 16))
N_EXPERTS = 16
EC_CAPACITY = 2
EXPERT_FF = 2 * D_MODEL

LANES = 128
MXU_DIM = 256
VMEM_LIMIT = 56 * 1024 * 1024
V_ONES = 16
ATT_BLK = MXU_DIM
NEG_BIG = -1e30
LOG2E = math.log2(math.e)


def _cparams(sem):
    return pltpu.CompilerParams(dimension_semantics=sem, vmem_limit_bytes=VMEM_LIMIT)


_NT = (((1,), (1,)), ((), ()))


def _rms_rows(x, g):
    return x * lax.rsqrt(jnp.mean(x * x, axis=-1, keepdims=True) + RMS_EPS) * g


def _head_rms(t, bd, gain, hd):
    outs = []
    for j in range(t.shape[1] // MXU_DIM):
        tj = t[:, j * MXU_DIM:(j + 1) * MXU_DIM]
        ss = jnp.dot((tj * tj).astype(BF16), bd, preferred_element_type=F32)
        outs.append(tj * lax.rsqrt(ss * (1.0 / hd) + RMS_EPS))
    y = outs[0] if len(outs) == 1 else jnp.concatenate(outs, axis=-1)
    return y * gain


def _rope_slabs(y, cos, sin, half):
    lane = lax.broadcasted_iota(jnp.int32, (1, LANES), 1)
    first = (lane % (2 * half)) < half
    outs = []
    for j in range(y.shape[1] // LANES):
        yj = y[:, j * LANES:(j + 1) * LANES]
        swapped = jnp.where(first, pltpu.roll(yj, LANES - half, 1), pltpu.roll(yj, half, 1))
        outs.append(yj * cos + swapped * sin)
    return outs[0] if len(outs) == 1 else jnp.concatenate(outs, axis=-1)


def _store_vt(vt_ref, vt, *, heads, dv):
    hv = dv + V_ONES
    ones = jnp.ones((V_ONES, ATT_BLK), BF16)
    vt = vt.astype(BF16)
    for j in range(vt.shape[1] // ATT_BLK):
        for h in range(heads):
            vt_ref[0, j, h * hv:h * hv + dv, :] = vt[h * dv:(h + 1) * dv, j * ATT_BLK:(j + 1) * ATT_BLK]
            vt_ref[0, j, h * hv + dv:(h + 1) * hv, :] = ones


def _qkv_kernel(x_ref, g_ref, w_ref, wvt_ref, bd_ref, gain_ref, cos_ref, sin_ref, q_ref, k_ref, vt_ref,
                *, nq, half, q_scale, v_heads, dv):
    xn = _rms_rows(x_ref[...], g_ref[...]).astype(BF16)
    y = jnp.dot(xn, w_ref[...], preferred_element_type=F32)
    bd = bd_ref[...]
    cos, sin = cos_ref[...], sin_ref[...]
    gain = gain_ref[...]
    q = _rope_slabs(_head_rms(y[:, :nq], bd, gain[:, :nq], HEAD_DIM), cos, sin, half) * q_scale
    k = _rope_slabs(_head_rms(y[:, nq:], bd, gain[:, nq:], HEAD_DIM), cos, sin, half)
    q_ref[...] = q.astype(BF16)
    k_ref[...] = k.astype(BF16)
    vt = lax.dot_general(wvt_ref[...], xn, _NT, preferred_element_type=F32)
    _store_vt(vt_ref, vt, heads=v_heads, dv=dv)


def _vt_out(n, seq, tm, vr):
    nt = seq // tm
    spec = pl.BlockSpec((1, tm // ATT_BLK, vr, ATT_BLK), lambda i: (i // nt, i % nt, 0, 0))
    return spec, jax.ShapeDtypeStruct((n // seq, seq // ATT_BLK, vr, ATT_BLK), BF16)


def _qkv_proj(x, g, w, wvt, gain, cos, sin, bd, *, nq, nk, half, q_scale, v_heads, dv, seq, tm):
    n, d = x.shape
    nt = seq // tm
    row = lambda i: (i, 0)
    const = lambda i: (0, 0)
    vt_spec, vt_shape = _vt_out(n, seq, tm, v_heads * (dv + V_ONES))
    return pl.pallas_call(
        functools.partial(_qkv_kernel, nq=nq, half=half, q_scale=q_scale, v_heads=v_heads, dv=dv),
        grid=(n // tm,),
        in_specs=[pl.BlockSpec((tm, d), row), pl.BlockSpec((1, d), const),
                  pl.BlockSpec(w.shape, const), pl.BlockSpec(wvt.shape, const), pl.BlockSpec(bd.shape, const),
                  pl.BlockSpec(gain.shape, const),
                  pl.BlockSpec((tm, LANES), lambda i: (i % nt, 0)),
                  pl.BlockSpec((tm, LANES), lambda i: (i % nt, 0))],
        out_specs=[pl.BlockSpec((tm, nq), row), pl.BlockSpec((tm, nk), row), vt_spec],
        out_shape=[jax.ShapeDtypeStruct((n, nq), BF16), jax.ShapeDtypeStruct((n, nk), BF16), vt_shape],
        compiler_params=_cparams(("parallel",)),
        name="qkv_proj",
    )(x, g, w, wvt, bd, gain, cos, sin)


def _mla_kernel(x_ref, g_ref, wd_ref, qlg_ref, kvlg_ref, wuq_ref, wuk_ref, wuvt_ref, bd64_ref, bd32_ref,
                gq_ref, gk_ref, cos_ref, sin_ref, q_ref, k_ref, vt_ref, *, scale):
    xn = _rms_rows(x_ref[...], g_ref[...]).astype(BF16)
    c = jnp.dot(xn, wd_ref[...], preferred_element_type=F32)
    cq = _rms_rows(c[:, :B_Q_LORA], qlg_ref[...]).astype(BF16)
    ckv = _rms_rows(c[:, B_Q_LORA:B_Q_LORA + B_KV_LORA], kvlg_ref[...]).astype(BF16)
    q = jnp.dot(cq, wuq_ref[...], preferred_element_type=F32)
    kn = jnp.dot(ckv, wuk_ref[...], preferred_element_type=F32)
    bd64, bd32 = bd64_ref[...], bd32_ref[...]
    cos, sin = cos_ref[...], sin_ref[...]
    gq, gk = gq_ref[...], gk_ref[...]
    hw = B_HEADS * B_NOPE
    qn = _head_rms(q[:, :hw], bd64, gq[:, :hw], B_NOPE) * scale
    qr = _rope_slabs(_head_rms(q[:, hw:], bd32, gq[:, hw:], B_ROPE), cos, sin, B_ROPE // 2) * scale
    kn = _head_rms(kn, bd64, gk[:, :hw], B_NOPE)
    kr_raw = c[:, B_Q_LORA + B_KV_LORA:]
    kr2 = _head_rms(jnp.concatenate([kr_raw, kr_raw], axis=-1), bd32,
                    jnp.concatenate([gk[:, hw:], gk[:, hw:]], axis=-1), B_ROPE)
    kr = _rope_slabs(kr2[:, :LANES], cos, sin, B_ROPE // 2).astype(BF16)
    qn, qr, kn = qn.astype(BF16), qr.astype(BF16), kn.astype(BF16)
    for p in range(B_HEADS // 2):
        sl = slice(p * LANES, (p + 1) * LANES)
        q_ref[:, 2 * p * LANES:(2 * p + 1) * LANES] = qn[:, sl]
        q_ref[:, (2 * p + 1) * LANES:(2 * p + 2) * LANES] = qr[:, sl]
        k_ref[:, 2 * p * LANES:(2 * p + 1) * LANES] = kn[:, sl]
        k_ref[:, (2 * p + 1) * LANES:(2 * p + 2) * LANES] = kr
    vt = lax.dot_general(wuvt_ref[...], ckv, _NT, preferred_element_type=F32)
    _store_vt(vt_ref, vt, heads=B_HEADS, dv=B_V)


def _mla_proj(x, g, wd, qlg, kvlg, wuq, wuk, wuvt, bd64, bd32, gq, gk, cos, sin, *, scale, seq, tm):
    n, d = x.shape
    nt = seq // tm
    row = lambda i: (i, 0)
    const = lambda i: (0, 0)
    full = lambda a: pl.BlockSpec(a.shape, const)
    wq = (B_HEADS // 2) * 2 * LANES
    vt_spec, vt_shape = _vt_out(n, seq, tm, B_HEADS * (B_V + V_ONES))
    return pl.pallas_call(
        functools.partial(_mla_kernel, scale=scale),
        grid=(n // tm,),
        in_specs=[pl.BlockSpec((tm, d), row), full(g), full(wd), full(qlg), full(kvlg), full(wuq), full(wuk),
                  full(wuvt), full(bd64), full(bd32), full(gq), full(gk),
                  pl.BlockSpec((tm, LANES), lambda i: (i % nt, 0)),
                  pl.BlockSpec((tm, LANES), lambda i: (i % nt, 0))],
        out_specs=[pl.BlockSpec((tm, wq), row), pl.BlockSpec((tm, wq), row), vt_spec],
        out_shape=[jax.ShapeDtypeStruct((n, wq), BF16), jax.ShapeDtypeStruct((n, wq), BF16), vt_shape],
        compiler_params=_cparams(("parallel",)),
        name="mla_proj",
    )(x, g, wd, qlg, kvlg, wuq, wuk, wuvt, bd64, bd32, gq, gk, cos, sin)


def _attn_kernel(*refs, mode, qw, dv, nv, tq, nk, band):
    it = iter(refs)
    q_ref, k_ref, vt_ref = next(it), next(it), next(it)
    bias_ref = next(it) if band is not None else None
    if mode == "diff":
        lam_ref, subg_ref = next(it), next(it)
    o_ref = next(it)
    qm_sc, m_sc, acc_sc, sa_sc, sb_sc = next(it), next(it), next(it), next(it), next(it)

    blk = ATT_BLK
    hv = dv + V_ONES
    nsub = tq // blk
    chains = [(r, h) for r in range(nsub) for h in range(2)]
    qi = pl.program_id(2)

    lane = lax.broadcasted_iota(jnp.int32, (1, qw), 1)
    if qw == LANES:
        masks = (lane < HEAD_DIM, lane >= HEAD_DIM)
    else:
        masks = ((lane < HEAD_DIM) | ((lane >= LANES) & (lane < LANES + B_ROPE)),
                 ((lane >= HEAD_DIM) & (lane < LANES)) | ((lane >= LANES + B_ROPE) & (lane < LANES + 2 * B_ROPE)))
    for ci, (r, h) in enumerate(chains):
        qr = q_ref[0, r * blk:(r + 1) * blk, :]
        qm_sc[ci] = jnp.where(masks[h], qr, jnp.zeros_like(qr))
    m_sc[...] = jnp.full(m_sc.shape, NEG_BIG, F32)
    acc_sc[...] = jnp.zeros(acc_sc.shape, F32)

    def scores(c, s_sc):
        start = pl.multiple_of(c * blk, blk)
        k = k_ref[0, pl.ds(start, blk), :]
        for ci in range(len(chains)):
            st = lax.dot_general(k, qm_sc[ci], _NT, preferred_element_type=F32)
            if band is not None:
                st = st + bias_ref[c - qi + band]
            s_sc[ci] = st

    def consume(c, s_sc):
        vt = vt_ref[0, c]
        ps, alphas = [], []
        for ci in range(len(chains)):
            st = s_sc[ci]
            m_old = m_sc[ci]
            m_new = jnp.maximum(m_old, jnp.max(st, axis=0, keepdims=True))
            ps.append(jnp.exp2(st - m_new).astype(BF16))
            alphas.append(jnp.exp2(m_old - m_new))
            m_sc[ci] = m_new
        for ci, (r, h) in enumerate(chains):
            voff = (h if nv == 2 else 0) * hv
            pv = jnp.dot(vt[voff:voff + hv], ps[ci], preferred_element_type=F32)
            acc_sc[ci] = acc_sc[ci] * alphas[ci] + pv

    if band is None:
        lo, hi = 0, nk
    else:
        lo, hi = jnp.maximum(qi - band, 0), jnp.minimum(qi + band + 1, nk)
    npairs = (hi - lo - 1) // 2

    def body(i, carry):
        c = lo + 2 * i
        scores(c + 1, sb_sc)
        consume(c, sa_sc)
        scores(c + 2, sa_sc)
        consume(c + 1, sb_sc)
        return carry

    scores(lo, sa_sc)
    lax.fori_loop(0, npairs, body, 0)
    c_tail = lo + 2 * npairs
    if band is None:
        if (hi - lo) % 2 == 0:
            scores(c_tail + 1, sb_sc)
            consume(c_tail, sa_sc)
            consume(c_tail + 1, sb_sc)
        else:
            consume(c_tail, sa_sc)
    else:
        two_left = (hi - c_tail) == 2

        @pl.when(two_left)
        def _():
            scores(c_tail + 1, sb_sc)

        consume(c_tail, sa_sc)

        @pl.when(two_left)
        def _():
            consume(c_tail + 1, sb_sc)

    for r in range(nsub):
        a, b = acc_sc[2 * r], acc_sc[2 * r + 1]
        oa = a[:dv] / a[dv:dv + 1]
        ob = b[:dv] / b[dv:dv + 1]
        if mode == "pair":
            o = jnp.concatenate([oa, ob], axis=0).T
        else:
            o = (oa - lam_ref[0, 0] * ob).T
            o = _rms_rows(o, subg_ref[...]) * (1.0 - lam_ref[0, 1])
        o_ref[0, r * blk:(r + 1) * blk, :] = o.astype(o_ref.dtype)


def _attention(q, k, vt, *, mode, qw, dv, nv, kmap, vmap, tq, bias=None, band=None, lam=None, subg=None):
    b, s, cq = q.shape
    pairs = cq // qw
    nk = s // ATT_BLK
    hv = dv + V_ONES
    in_specs = [pl.BlockSpec((1, tq, qw), lambda bi, p, i: (bi, i, p)),
                pl.BlockSpec((1, s, qw), lambda bi, p, i: (bi, 0, kmap(p))),
                pl.BlockSpec((1, nk, nv * hv, ATT_BLK), lambda bi, p, i: (bi, 0, vmap(p), 0))]
    args = [q, k, vt]
    if band is not None:
        in_specs.append(pl.BlockSpec(bias.shape, lambda bi, p, i: (0, 0, 0)))
        args.append(bias)
    if mode == "diff":
        in_specs.append(pl.BlockSpec(memory_space=pltpu.SMEM))
        in_specs.append(pl.BlockSpec(subg.shape, lambda bi, p, i: (0, 0)))
        args += [lam, subg]
    nchains = 2 * (tq // ATT_BLK)
    return pl.pallas_call(
        functools.partial(_attn_kernel, mode=mode, qw=qw, dv=dv, nv=nv, tq=tq, nk=nk, band=band),
        grid=(b, pairs, s // tq),
        in_specs=in_specs,
        out_specs=pl.BlockSpec((1, tq, LANES), lambda bi, p, i: (bi, i, p)),
        out_shape=jax.ShapeDtypeStruct((b, s, pairs * LANES), BF16),
        scratch_shapes=[pltpu.VMEM((nchains, ATT_BLK, qw), BF16), pltpu.VMEM((nchains, 1, ATT_BLK), F32),
                        pltpu.VMEM((nchains, hv, ATT_BLK), F32),
                        pltpu.VMEM((nchains, ATT_BLK, ATT_BLK), F32), pltpu.VMEM((nchains, ATT_BLK, ATT_BLK), F32)],
        compiler_params=_cparams(("parallel", "parallel", "arbitrary")),
        name="attention_" + mode,
    )(*args)


def _oproj_kernel(o_ref, w_ref, x_ref, g_ref, wr_ref, xnew_ref, xn_ref, aff_ref):
    xnew = x_ref[...] + jnp.dot(o_ref[...], w_ref[...], preferred_element_type=F32)
    xnew_ref[...] = xnew
    xn = _rms_rows(xnew, g_ref[...]).astype(BF16)
    xn_ref[...] = xn
    logits = lax.dot_general(wr_ref[...], xn, _NT, preferred_element_type=F32)
    z = logits - jnp.max(logits, axis=0, keepdims=True)
    e = jnp.exp(z)
    aff_ref[...] = e / jnp.sum(e, axis=0, keepdims=True)


def _oproj(o, w, x, g, wr_t, *, tm):
    n, d = x.shape
    row = lambda i: (i, 0)
    const = lambda i: (0, 0)
    return pl.pallas_call(
        _oproj_kernel,
        grid=(n // tm,),
        in_specs=[pl.BlockSpec((tm, o.shape[1]), row), pl.BlockSpec(w.shape, const), pl.BlockSpec((tm, d), row),
                  pl.BlockSpec((1, d), const), pl.BlockSpec(wr_t.shape, const)],
        out_specs=[pl.BlockSpec((tm, d), row), pl.BlockSpec((tm, d), row),
                   pl.BlockSpec((N_EXPERTS, tm), lambda i: (0, i))],
        out_shape=[jax.ShapeDtypeStruct((n, d), F32), jax.ShapeDtypeStruct((n, d), BF16),
                   jax.ShapeDtypeStruct((N_EXPERTS, n), F32)],
        compiler_params=_cparams(("parallel",)),
        name="oproj_router",
    )(o, w, x, g, wr_t)


def _ffn_kernel(xg_ref, wg_ref, wu_ref, wd_ref, gate_ref, y_ref):
    xg = xg_ref[0]
    a = jnp.dot(xg, wg_ref[0], preferred_element_type=F32)
    u = jnp.dot(xg, wu_ref[0], preferred_element_type=F32)
    h = (a * jax.nn.sigmoid(a) * u).astype(BF16)
    y_ref[0] = jnp.dot(h, wd_ref[0], preferred_element_type=F32) * gate_ref[0]


def _expert_ffn(xg, wg, wu, wd, gate, *, tm):
    e, cap, d = xg.shape
    ff = wg.shape[2]
    return pl.pallas_call(
        _ffn_kernel,
        grid=(e, cap // tm),
        in_specs=[pl.BlockSpec((1, tm, d), lambda ei, i: (ei, i, 0)),
                  pl.BlockSpec((1, d, ff), lambda ei, i: (ei, 0, 0)),
                  pl.BlockSpec((1, d, ff), lambda ei, i: (ei, 0, 0)),
                  pl.BlockSpec((1, ff, d), lambda ei, i: (ei, 0, 0)),
                  pl.BlockSpec((1, tm, 1), lambda ei, i: (ei, i, 0))],
        out_specs=pl.BlockSpec((1, tm, d), lambda ei, i: (ei, i, 0)),
        out_shape=jax.ShapeDtypeStruct((e, cap, d), F32),
        compiler_params=_cparams(("parallel", "arbitrary")),
        name="expert_ffn",
    )(xg, wg, wu, wd, gate)


def _block_diag(hd):
    i = jnp.arange(MXU_DIM)
    return (i[:, None] // hd == i[None, :] // hd).astype(BF16)


def _rope_tables_std(seq, dim):
    half = dim // 2
    inv_freq = ROPE_THETA ** (-jnp.arange(half, dtype=F32) * 2.0 / dim)
    ang = jnp.arange(seq).astype(F32)[:, None] * inv_freq[None, :]
    cos, sin = jnp.cos(ang), jnp.sin(ang)
    reps = LANES // dim
    return (jnp.tile(jnp.concatenate([cos, cos], axis=-1), (1, reps)),
            jnp.tile(jnp.concatenate([-sin, sin], axis=-1), (1, reps)))


def _rope_tables_axial(seq):
    sub = HEAD_DIM // 2
    half = sub // 2
    inv_freq = ROPE_THETA ** (-jnp.arange(half, dtype=F32) * 2.0 / sub)
    n_rows = seq // GRID_W
    rows = jnp.repeat(jnp.arange(n_rows), GRID_W).astype(F32)
    cols = jnp.tile(jnp.arange(GRID_W), n_rows).astype(F32)
    ar, ac = rows[:, None] * inv_freq[None, :], cols[:, None] * inv_freq[None, :]
    cos = jnp.concatenate([jnp.cos(ar), jnp.cos(ar), jnp.cos(ac), jnp.cos(ac)], axis=-1)
    sin = jnp.concatenate([-jnp.sin(ar), jnp.sin(ar), -jnp.sin(ac), jnp.sin(ac)], axis=-1)
    return jnp.tile(cos, (1, 2)), jnp.tile(sin, (1, 2))


def _dilated_bias(band):
    t = ATT_BLK
    rel = jnp.arange(-band, band + 1)[:, None, None] * t
    d = rel + jnp.arange(t)[None, :, None] - jnp.arange(t)[None, None, :]
    cnt = jnp.zeros(d.shape, F32)
    for window, dil in DILATED_BRANCHES:
        cnt = cnt + ((d % dil == 0) & (jnp.abs(d) <= window // 2)).astype(F32)
    return jnp.where(cnt > 0, jnp.log2(jnp.maximum(cnt, 1.0)), NEG_BIG)


def _tile_gain(g, reps):
    return jnp.tile(g.astype(F32), reps)[None, :]


def _pick(n, pref):
    t = min(n, pref)
    while n % t:
        t //= 2
    return t


def _mixer_a(x, b, s, g, w_qkv, q_g, k_g):
    nq = A_HEADS * HEAD_DIM
    nkv = A_KV_HEADS * HEAD_DIM
    wq, wk, wv = w_qkv[:, :nq], w_qkv[:, nq:nq + nkv], w_qkv[:, nq + nkv:]
    wk2 = jnp.repeat(wk.reshape(D_MODEL, A_KV_HEADS, 1, HEAD_DIM), 2, axis=2).reshape(D_MODEL, -1)
    w = jnp.concatenate([wq, wk2], axis=1).astype(BF16)
    nk = 2 * nkv
    gain = jnp.concatenate([_tile_gain(q_g, nq // HEAD_DIM), _tile_gain(k_g, nk // HEAD_DIM)], axis=1)
    cos, sin = _rope_tables_axial(s)
    q, k, vt = _qkv_proj(x, g, w, wv.T.astype(BF16), gain, cos, sin, _block_diag(HEAD_DIM), nq=nq, nk=nk,
                         half=HEAD_DIM // 4, q_scale=HEAD_DIM ** -0.5 * LOG2E, v_heads=A_KV_HEADS, dv=HEAD_DIM,
                         seq=s, tm=_pick(s, 512))
    group_pairs = (A_HEADS // A_KV_HEADS) // 2
    return _attention(q.reshape(b, s, -1), k.reshape(b, s, -1), vt, mode="pair", qw=LANES, dv=HEAD_DIM, nv=1,
                      kmap=lambda p: p // group_pairs, vmap=lambda p: p // group_pairs, tq=_pick(s, 512))


def _mixer_b(x, b, s, g, w_down, q_lat_g, kv_lat_g, w_uq, w_ukv, q_nope_g, q_rope_g, k_nope_g, k_rope_g):
    zpad = lambda a, n: jnp.concatenate([a, jnp.zeros(a.shape[:-1] + (n,), a.dtype)], axis=-1)
    w_kr = w_down[:, B_Q_LORA + B_KV_LORA:]
    wd = jnp.concatenate([w_down[:, :B_Q_LORA + B_KV_LORA], zpad(jnp.concatenate([w_kr, w_kr], axis=1), 2 * B_ROPE)],
                         axis=1).astype(BF16)
    uq = w_uq.reshape(B_Q_LORA, B_HEADS, B_NOPE + B_ROPE)
    uq_n = uq[:, :, :B_NOPE].reshape(B_Q_LORA, -1)
    uq_r = zpad(uq[:, :, B_NOPE:].reshape(B_Q_LORA, B_HEADS // 2, 2 * B_ROPE), LANES - 2 * B_ROPE).reshape(B_Q_LORA, -1)
    wuq = jnp.concatenate([uq_n, uq_r], axis=1).astype(BF16)
    ukv = w_ukv.reshape(B_KV_LORA, B_HEADS, B_NOPE + B_V)
    wuk = ukv[:, :, :B_NOPE].reshape(B_KV_LORA, -1).astype(BF16)
    wuvt = ukv[:, :, B_NOPE:].reshape(B_KV_LORA, -1).T.astype(BF16)
    gq = jnp.concatenate([_tile_gain(q_nope_g, B_HEADS), _tile_gain(q_rope_g, B_HEADS * B_NOPE // B_ROPE)], axis=1)
    gk = jnp.concatenate([_tile_gain(k_nope_g, B_HEADS), _tile_gain(k_rope_g, LANES // B_ROPE)], axis=1)
    cos, sin = _rope_tables_std(s, B_ROPE)
    q, k, vt = _mla_proj(x, g, wd, q_lat_g.astype(F32)[None, :], kv_lat_g.astype(F32)[None, :], wuq, wuk, wuvt,
                         _block_diag(B_NOPE), _block_diag(B_ROPE), gq, gk, cos, sin,
                         scale=(B_NOPE + B_ROPE) ** -0.5 * LOG2E, seq=s, tm=_pick(s, 512))
    return _attention(q.reshape(b, s, -1), k.reshape(b, s, -1), vt, mode="pair", qw=2 * LANES, dv=B_V, nv=2,
                      kmap=lambda p: p, vmap=lambda p: p, tq=_pick(s, 512))


def _mixer_c(x, b, s, g, w_qkv, q_g, k_g, lq1, lk1, lq2, lk2, subln_g, layer_idx):
    nq = 2 * C_HEADS * HEAD_DIM
    gain = jnp.concatenate([_tile_gain(q_g, nq // HEAD_DIM), _tile_gain(k_g, nq // HEAD_DIM)], axis=1)
    cos, sin = _rope_tables_std(s, HEAD_DIM)
    q, k, vt = _qkv_proj(x, g, w_qkv[:, :2 * nq].astype(BF16), w_qkv[:, 2 * nq:].T.astype(BF16), gain, cos, sin,
                         _block_diag(HEAD_DIM), nq=nq, nk=nq, half=HEAD_DIM // 2, q_scale=HEAD_DIM ** -0.5 * LOG2E,
                         v_heads=C_HEADS, dv=2 * HEAD_DIM, seq=s, tm=_pick(s, 512))
    lam_init = LAMBDA_INIT_BASE - LAMBDA_INIT_AMP * math.exp(-LAMBDA_INIT_RATE * layer_idx)
    lam = (jnp.exp(jnp.sum(lq1.astype(F32) * lk1.astype(F32))) - jnp.exp(jnp.sum(lq2.astype(F32) * lk2.astype(F32)))
           + lam_init)
    lam_arr = jnp.stack([lam, jnp.asarray(lam_init, F32)]).reshape(1, 2).astype(F32)
    return _attention(q.reshape(b, s, -1), k.reshape(b, s, -1), vt, mode="diff", qw=LANES, dv=2 * HEAD_DIM, nv=1,
                      kmap=lambda p: p, vmap=lambda p: p, tq=_pick(s, 512), lam=lam_arr,
                      subg=subln_g.astype(F32)[None, :])


def _mixer_d(x, b, s, g, w_qkv, q_g, k_g):
    nq = D_HEADS * HEAD_DIM
    gain = jnp.concatenate([_tile_gain(q_g, D_HEADS), _tile_gain(k_g, D_HEADS)], axis=1)
    cos, sin = _rope_tables_std(s, HEAD_DIM)
    q, k, vt = _qkv_proj(x, g, w_qkv[:, :2 * nq].astype(BF16), w_qkv[:, 2 * nq:].T.astype(BF16), gain, cos, sin,
                         _block_diag(HEAD_DIM), nq=nq, nk=nq, half=HEAD_DIM // 2, q_scale=HEAD_DIM ** -0.5 * LOG2E,
                         v_heads=D_HEADS, dv=HEAD_DIM, seq=s, tm=_pick(s, 512))
    reach = max(w // 2 for w, _ in DILATED_BRANCHES)
    band = -(-reach // ATT_BLK)
    return _attention(q.reshape(b, s, -1), k.reshape(b, s, -1), vt, mode="pair", qw=LANES, dv=HEAD_DIM, nv=2,
                      kmap=lambda p: p, vmap=lambda p: p, tq=ATT_BLK, bias=_dilated_bias(band), band=band)


def _ec_ffn(xnew, xn, aff, wg, wu, wd):
    n, d = xnew.shape
    cap = (EC_CAPACITY * n) // N_EXPERTS
    gate, idx = lax.top_k(aff, cap)
    xg = xn[idx]
    y = _expert_ffn(xg, wg, wu, wd, gate[..., None], tm=_pick(cap, 512))
    return xnew.at[idx.reshape(-1)].add(y.reshape(-1, d))


def kernel(x_prompt, x_sample, norm_mix_g, norm_ffn_g, a_w_qkv, a_q_norm_g, a_k_norm_g, a_w_o, b_w_down, b_q_lat_norm_g, b_kv_lat_norm_g, b_w_uq, b_w_ukv, b_q_nope_norm_g, b_q_rope_norm_g, b_k_nope_norm_g, b_k_rope_norm_g, b_w_o, c_w_qkv, c_q_norm_g, c_k_norm_g, c_lambda_q1, c_lambda_k1, c_lambda_q2, c_lambda_k2, c_subln_g, c_w_o, d_w_qkv, d_q_norm_g, d_k_norm_g, d_w_o, ec_w_router, ec_w_gate, ec_w_up, ec_w_down):
    depth = norm_mix_g.shape[0]
    wg_all, wu_all, wd_all = ec_w_gate.astype(BF16), ec_w_up.astype(BF16), ec_w_down.astype(BF16)

    def trunk(x3):
        b, s, d = x3.shape
        x = x3.reshape(b * s, d)
        for i in range(depth):
            m, j = i % N_MIXERS, i // N_MIXERS
            g = norm_mix_g[i].astype(F32)[None, :]
            if m == 0:
                o, w_o = _mixer_a(x, b, s, g, a_w_qkv[j], a_q_norm_g[j], a_k_norm_g[j]), a_w_o[j]
            elif m == 1:
                o = _mixer_b(x, b, s, g, b_w_down[j], b_q_lat_norm_g[j], b_kv_lat_norm_g[j], b_w_uq[j], b_w_ukv[j],
                             b_q_nope_norm_g[j], b_q_rope_norm_g[j], b_k_nope_norm_g[j], b_k_rope_norm_g[j])
                w_o = b_w_o[j]
            elif m == 2:
                o = _mixer_c(x, b, s, g, c_w_qkv[j], c_q_norm_g[j], c_k_norm_g[j], c_lambda_q1[j], c_lambda_k1[j],
                             c_lambda_q2[j], c_lambda_k2[j], c_subln_g[j], i)
                w_o = c_w_o[j]
            else:
                o, w_o = _mixer_d(x, b, s, g, d_w_qkv[j], d_q_norm_g[j], d_k_norm_g[j]), d_w_o[j]
            xnew, xn, aff = _oproj(o.reshape(b * s, -1), w_o.astype(BF16), x, norm_ffn_g[i].astype(F32)[None, :],
                                   ec_w_router[i].T.astype(BF16), tm=_pick(b * s, 512))
            x = _ec_ffn(xnew, xn, aff, wg_all[i], wu_all[i], wd_all[i])
        return x.reshape(b, s, d)

    return (trunk(x_prompt), trunk(x_sample))
```

```python
import functools
import math

import jax
import jax.numpy as jnp
from jax import lax
from jax.experimental import pallas as pl
from jax.experimental.pallas import tpu as pltpu

F32 = jnp.float32
BF16 = jnp.bfloat16

D_MODEL = 1024
HEAD_DIM = 64
GRID_W = 64
ROPE_THETA = 10000.0
RMS_EPS = 1e-6
N_MIXERS = 4
A_HEADS, A_KV_HEADS = 16, 4
B_HEADS, B_Q_LORA, B_KV_LORA, B_NOPE, B_ROPE, B_V = 16, 384, 256, 64, 32, 64
C_HEADS = 8
LAMBDA_INIT_BASE, LAMBDA_INIT_AMP, LAMBDA_INIT_RATE = 0.8, 0.6, 0.3
D_HEADS = 16
DILATED_BRANCHES = ((128, 1), (512, 4), (2048, 16))
N_EXPERTS = 16
EC_CAPACITY = 2
EXPERT_FF = 2 * D_MODEL

LANES = 128
MXU_DIM = 256
VMEM_LIMIT = 56 * 1024 * 1024
V_ONES = 16
ATT_BLK = MXU_DIM
ATT_TQ = 4 * ATT_BLK
NEG_BIG = -1e30
LOG2E = math.log2(math.e)


def _cparams(sem):
    return pltpu.CompilerParams(dimension_semantics=sem, vmem_limit_bytes=VMEM_LIMIT)


_NT = (((1,), (1,)), ((), ()))


def _rms_rows(x, g):
    return x * lax.rsqrt(jnp.mean(x * x, axis=-1, keepdims=True) + RMS_EPS) * g


def _head_rms(t, bd, gain, hd):
    outs = []
    for j in range(t.shape[1] // MXU_DIM):
        tj = t[:, j * MXU_DIM:(j + 1) * MXU_DIM]
        ss = jnp.dot((tj * tj).astype(BF16), bd, preferred_element_type=F32)
        outs.append(tj * lax.rsqrt(ss * (1.0 / hd) + RMS_EPS))
    y = outs[0] if len(outs) == 1 else jnp.concatenate(outs, axis=-1)
    return y * gain


def _rope_slabs(y, cos, sin, half):
    lane = lax.broadcasted_iota(jnp.int32, (1, LANES), 1)
    first = (lane % (2 * half)) < half
    outs = []
    for j in range(y.shape[1] // LANES):
        yj = y[:, j * LANES:(j + 1) * LANES]
        swapped = jnp.where(first, pltpu.roll(yj, LANES - half, 1), pltpu.roll(yj, half, 1))
        outs.append(yj * cos + swapped * sin)
    return outs[0] if len(outs) == 1 else jnp.concatenate(outs, axis=-1)


def _store_vt(vt_ref, vt, *, heads, dv):
    hv = dv + V_ONES
    ones = jnp.ones((V_ONES, ATT_BLK), BF16)
    vt = vt.astype(BF16)
    for j in range(vt.shape[1] // ATT_BLK):
        for h in range(heads):
            vt_ref[0, j, h * hv:h * hv + dv, :] = vt[h * dv:(h + 1) * dv, j * ATT_BLK:(j + 1) * ATT_BLK]
            vt_ref[0, j, h * hv + dv:(h + 1) * hv, :] = ones


def _qkv_kernel(x_ref, g_ref, w_ref, wvt_ref, bd_ref, gain_ref, cos_ref, sin_ref, q_ref, k_ref, vt_ref,
                *, nq, half, q_scale, v_heads, dv):
    xn = _rms_rows(x_ref[...], g_ref[...]).astype(BF16)
    y = jnp.dot(xn, w_ref[...], preferred_element_type=F32)
    bd = bd_ref[...]
    cos, sin = cos_ref[...], sin_ref[...]
    gain = gain_ref[...]
    q = _rope_slabs(_head_rms(y[:, :nq], bd, gain[:, :nq], HEAD_DIM), cos, sin, half) * q_scale
    k = _rope_slabs(_head_rms(y[:, nq:], bd, gain[:, nq:], HEAD_DIM), cos, sin, half)
    q_ref[...] = q.astype(BF16)
    k_ref[...] = k.astype(BF16)
    vt = lax.dot_general(wvt_ref[...], xn, _NT, preferred_element_type=F32)
    _store_vt(vt_ref, vt, heads=v_heads, dv=dv)


def _vt_out(n, seq, tm, vr):
    nt = seq // tm
    spec = pl.BlockSpec((1, tm // ATT_BLK, vr, ATT_BLK), lambda i: (i // nt, i % nt, 0, 0))
    return spec, jax.ShapeDtypeStruct((n // seq, seq // ATT_BLK, vr, ATT_BLK), BF16)


def _qkv_proj(x, g, w, wvt, gain, cos, sin, bd, *, nq, nk, half, q_scale, v_heads, dv, seq, tm):
    n, d = x.shape
    nt = seq // tm
    row = lambda i: (i, 0)
    const = lambda i: (0, 0)
    vt_spec, vt_shape = _vt_out(n, seq, tm, v_heads * (dv + V_ONES))
    return pl.pallas_call(
        functools.partial(_qkv_kernel, nq=nq, half=half, q_scale=q_scale, v_heads=v_heads, dv=dv),
        grid=(n // tm,),
        in_specs=[pl.BlockSpec((tm, d), row), pl.BlockSpec((1, d), const),
                  pl.BlockSpec(w.shape, const), pl.BlockSpec(wvt.shape, const), pl.BlockSpec(bd.shape, const),
                  pl.BlockSpec(gain.shape, const),
                  pl.BlockSpec((tm, LANES), lambda i: (i % nt, 0)),
                  pl.BlockSpec((tm, LANES), lambda i: (i % nt, 0))],
        out_specs=[pl.BlockSpec((tm, nq), row), pl.BlockSpec((tm, nk), row), vt_spec],
        out_shape=[jax.ShapeDtypeStruct((n, nq), BF16), jax.ShapeDtypeStruct((n, nk), BF16), vt_shape],
        compiler_params=_cparams(("parallel",)),
        name="qkv_proj",
    )(x, g, w, wvt, bd, gain, cos, sin)


def _mla_kernel(x_ref, g_ref, wd_ref, qlg_ref, kvlg_ref, wuq_ref, wuk_ref, wuvt_ref, bd64_ref, bd32_ref,
                gq_ref, gk_ref, cos_ref, sin_ref, q_ref, k_ref, vt_ref, *, scale):
    xn = _rms_rows(x_ref[...], g_ref[...]).astype(BF16)
    c = jnp.dot(xn, wd_ref[...], preferred_element_type=F32)
    cq = _rms_rows(c[:, :B_Q_LORA], qlg_ref[...]).astype(BF16)
    ckv = _rms_rows(c[:, B_Q_LORA:B_Q_LORA + B_KV_LORA], kvlg_ref[...]).astype(BF16)
    q = jnp.dot(cq, wuq_ref[...], preferred_element_type=F32)
    kn = jnp.dot(ckv, wuk_ref[...], preferred_element_type=F32)
    bd64, bd32 = bd64_ref[...], bd32_ref[...]
    cos, sin = cos_ref[...], sin_ref[...]
    gq, gk = gq_ref[...], gk_ref[...]
    hw = B_HEADS * B_NOPE
    qn = _head_rms(q[:, :hw], bd64, gq[:, :hw], B_NOPE) * scale
    qr = _rope_slabs(_head_rms(q[:, hw:], bd32, gq[:, hw:], B_ROPE), cos, sin, B_ROPE // 2) * scale
    kn = _head_rms(kn, bd64, gk[:, :hw], B_NOPE)
    kr_raw = c[:, B_Q_LORA + B_KV_LORA:]
    kr2 = _head_rms(jnp.concatenate([kr_raw, kr_raw], axis=-1), bd32,
                    jnp.concatenate([gk[:, hw:], gk[:, hw:]], axis=-1), B_ROPE)
    kr = _rope_slabs(kr2[:, :LANES], cos, sin, B_ROPE // 2).astype(BF16)
    qn, qr, kn = qn.astype(BF16), qr.astype(BF16), kn.astype(BF16)
    for p in range(B_HEADS // 2):
        sl = slice(p * LANES, (p + 1) * LANES)
        q_ref[:, 2 * p * LANES:(2 * p + 1) * LANES] = qn[:, sl]
        q_ref[:, (2 * p + 1) * LANES:(2 * p + 2) * LANES] = qr[:, sl]
        k_ref[:, 2 * p * LANES:(2 * p + 1) * LANES] = kn[:, sl]
        k_ref[:, (2 * p + 1) * LANES:(2 * p + 2) * LANES] = kr
    vt = lax.dot_general(wuvt_ref[...], ckv, _NT, preferred_element_type=F32)
    _store_vt(vt_ref, vt, heads=B_HEADS, dv=B_V)


def _mla_proj(x, g, wd, qlg, kvlg, wuq, wuk, wuvt, bd64, bd32, gq, gk, cos, sin, *, scale, seq, tm):
    n, d = x.shape
    nt = seq // tm
    row = lambda i: (i, 0)
    const = lambda i: (0, 0)
    full = lambda a: pl.BlockSpec(a.shape, const)
    wq = (B_HEADS // 2) * 2 * LANES
    vt_spec, vt_shape = _vt_out(n, seq, tm, B_HEADS * (B_V + V_ONES))
    return pl.pallas_call(
        functools.partial(_mla_kernel, scale=scale),
        grid=(n // tm,),
        in_specs=[pl.BlockSpec((tm, d), row), full(g), full(wd), full(qlg), full(kvlg), full(wuq), full(wuk),
                  full(wuvt), full(bd64), full(bd32), full(gq), full(gk),
                  pl.BlockSpec((tm, LANES), lambda i: (i % nt, 0)),
                  pl.BlockSpec((tm, LANES), lambda i: (i % nt, 0))],
        out_specs=[pl.BlockSpec((tm, wq), row), pl.BlockSpec((tm, wq), row), vt_spec],
        out_shape=[jax.ShapeDtypeStruct((n, wq), BF16), jax.ShapeDtypeStruct((n, wq), BF16), vt_shape],
        compiler_params=_cparams(("parallel",)),
        name="mla_proj",
    )(x, g, wd, qlg, kvlg, wuq, wuk, wuvt, bd64, bd32, gq, gk, cos, sin)


def _attn_kernel(*refs, mode, qw, dv, nv, tq, nk, band):
    it = iter(refs)
    q_ref, k_ref, vt_ref = next(it), next(it), next(it)
    bias_ref = next(it) if band is not None else None
    if mode == "diff":
        lam_ref, subg_ref = next(it), next(it)
    o_ref = next(it)
    qm_sc, m_sc, acc_sc, sa_sc, sb_sc = next(it), next(it), next(it), next(it), next(it)

    blk = ATT_BLK
    hv = dv + V_ONES
    nsub = tq // blk
    chains = [(r, h) for r in range(nsub) for h in range(2)]
    qi = pl.program_id(2)

    lane = lax.broadcasted_iota(jnp.int32, (1, qw), 1)
    if qw == LANES:
        masks = (lane < HEAD_DIM, lane >= HEAD_DIM)
    else:
        masks = ((lane < HEAD_DIM) | ((lane >= LANES) & (lane < LANES + B_ROPE)),
                 ((lane >= HEAD_DIM) & (lane < LANES)) | ((lane >= LANES + B_ROPE) & (lane < LANES + 2 * B_ROPE)))
    for ci, (r, h) in enumerate(chains):
        qr = q_ref[0, r * blk:(r + 1) * blk, :]
        qm_sc[ci] = jnp.where(masks[h], qr, jnp.zeros_like(qr))
    m_sc[...] = jnp.full(m_sc.shape, NEG_BIG, F32)
    acc_sc[...] = jnp.zeros(acc_sc.shape, F32)

    def scores(c, s_sc):
        start = pl.multiple_of(c * blk, blk)
        k = k_ref[0, pl.ds(start, blk), :]
        for ci in range(len(chains)):
            st = lax.dot_general(k, qm_sc[ci], _NT, preferred_element_type=F32)
            if band is not None:
                rel = c - (qi * nsub + chains[ci][0]) + band + 1
                st = st + bias_ref[jnp.clip(rel, 0, 2 * band + 2)]
            s_sc[ci] = st

    def consume(c, s_sc):
        vt = vt_ref[0, c]
        ps, alphas = [], []
        for ci in range(len(chains)):
            st = s_sc[ci]
            m_old = m_sc[ci]
            m_new = jnp.maximum(m_old, jnp.max(st, axis=0, keepdims=True))
            ps.append(jnp.exp2(st - m_new).astype(BF16))
            alphas.append(jnp.exp2(m_old - m_new))
            m_sc[ci] = m_new
        for ci, (r, h) in enumerate(chains):
            voff = (h if nv == 2 else 0) * hv
            pv = jnp.dot(vt[voff:voff + hv], ps[ci], preferred_element_type=F32)
            acc_sc[ci] = acc_sc[ci] * alphas[ci] + pv

    if band is None:
        lo, hi = 0, nk
    else:
        lo, hi = jnp.maximum(qi * nsub - band, 0), jnp.minimum((qi + 1) * nsub + band, nk)
    npairs = (hi - lo - 1) // 2

    def body(i, carry):
        c = lo + 2 * i
        scores(c + 1, sb_sc)
        consume(c, sa_sc)
        scores(c + 2, sa_sc)
        consume(c + 1, sb_sc)
        return carry

    scores(lo, sa_sc)
    lax.fori_loop(0, npairs, body, 0)
    c_tail = lo + 2 * npairs
    if band is None:
        if (hi - lo) % 2 == 0:
            scores(c_tail + 1, sb_sc)
            consume(c_tail, sa_sc)
            consume(c_tail + 1, sb_sc)
        else:
            consume(c_tail, sa_sc)
    else:
        two_left = (hi - c_tail) == 2

        @pl.when(two_left)
        def _():
            scores(c_tail + 1, sb_sc)

        consume(c_tail, sa_sc)

        @pl.when(two_left)
        def _():
            consume(c_tail + 1, sb_sc)

    for r in range(nsub):
        a, b = acc_sc[2 * r], acc_sc[2 * r + 1]
        oa = a[:dv] / a[dv:dv + 1]
        ob = b[:dv] / b[dv:dv + 1]
        if mode == "pair":
            o = jnp.concatenate([oa, ob], axis=0).T
        else:
            o = (oa - lam_ref[0, 0] * ob).T
            o = _rms_rows(o, subg_ref[...]) * (1.0 - lam_ref[0, 1])
        o_ref[0, r * blk:(r + 1) * blk, :] = o.astype(o_ref.dtype)


def _attention(q, k, vt, *, mode, qw, dv, nv, kmap, vmap, tq, bias=None, band=None, lam=None, subg=None):
    b, s, cq = q.shape
    pairs = cq // qw
    nk = s // ATT_BLK
    hv = dv + V_ONES
    in_specs = [pl.BlockSpec((1, tq, qw), lambda bi, p, i: (bi, i, p)),
                pl.BlockSpec((1, s, qw), lambda bi, p, i: (bi, 0, kmap(p))),
                pl.BlockSpec((1, nk, nv * hv, ATT_BLK), lambda bi, p, i: (bi, 0, vmap(p), 0))]
    args = [q, k, vt]
    if band is not None:
        in_specs.append(pl.BlockSpec(bias.shape, lambda bi, p, i: (0, 0, 0)))
        args.append(bias)
    if mode == "diff":
        in_specs.append(pl.BlockSpec(memory_space=pltpu.SMEM))
        in_specs.append(pl.BlockSpec(subg.shape, lambda bi, p, i: (0, 0)))
        args += [lam, subg]
    nchains = 2 * (tq // ATT_BLK)
    return pl.pallas_call(
        functools.partial(_attn_kernel, mode=mode, qw=qw, dv=dv, nv=nv, tq=tq, nk=nk, band=band),
        grid=(b, pairs, s // tq),
        in_specs=in_specs,
        out_specs=pl.BlockSpec((1, tq, LANES), lambda bi, p, i: (bi, i, p)),
        out_shape=jax.ShapeDtypeStruct((b, s, pairs * LANES), BF16),
        scratch_shapes=[pltpu.VMEM((nchains, ATT_BLK, qw), BF16), pltpu.VMEM((nchains, 1, ATT_BLK), F32),
                        pltpu.VMEM((nchains, hv, ATT_BLK), F32),
                        pltpu.VMEM((nchains, ATT_BLK, ATT_BLK), F32), pltpu.VMEM((nchains, ATT_BLK, ATT_BLK), F32)],
        compiler_params=_cparams(("parallel", "parallel", "arbitrary")),
        name="attention_" + mode,
    )(*args)


def _oproj_kernel(o_ref, w_ref, x_ref, g_ref, wr_ref, xnew_ref, xn_ref, aff_ref):
    xnew = x_ref[...] + jnp.dot(o_ref[...], w_ref[...], preferred_element_type=F32)
    xnew_ref[...] = xnew
    xn = _rms_rows(xnew, g_ref[...]).astype(BF16)
    xn_ref[...] = xn
    logits = lax.dot_general(wr_ref[...], xn, _NT, preferred_element_type=F32)
    z = logits - jnp.max(logits, axis=0, keepdims=True)
    e = jnp.exp(z)
    aff_ref[...] = e / jnp.sum(e, axis=0, keepdims=True)


def _oproj(o, w, x, g, wr_t, *, tm):
    n, d = x.shape
    row = lambda i: (i, 0)
    const = lambda i: (0, 0)
    return pl.pallas_call(
        _oproj_kernel,
        grid=(n // tm,),
        in_specs=[pl.BlockSpec((tm, o.shape[1]), row), pl.BlockSpec(w.shape, const), pl.BlockSpec((tm, d), row),
                  pl.BlockSpec((1, d), const), pl.BlockSpec(wr_t.shape, const)],
        out_specs=[pl.BlockSpec((tm, d), row), pl.BlockSpec((tm, d), row),
                   pl.BlockSpec((N_EXPERTS, tm), lambda i: (0, i))],
        out_shape=[jax.ShapeDtypeStruct((n, d), F32), jax.ShapeDtypeStruct((n, d), BF16),
                   jax.ShapeDtypeStruct((N_EXPERTS, n), F32)],
        compiler_params=_cparams(("parallel",)),
        name="oproj_router",
    )(o, w, x, g, wr_t)


def _ffn_kernel(xg_ref, wg_ref, wu_ref, wd_ref, gate_ref, y_ref):
    xg = xg_ref[0]
    a = jnp.dot(xg, wg_ref[0], preferred_element_type=F32)
    u = jnp.dot(xg, wu_ref[0], preferred_element_type=F32)
    h = (a * jax.nn.sigmoid(a) * u).astype(BF16)
    y_ref[0] = jnp.dot(h, wd_ref[0], preferred_element_type=F32) * gate_ref[0]


def _expert_ffn(xg, wg, wu, wd, gate, *, tm):
    e, cap, d = xg.shape
    ff = wg.shape[2]
    return pl.pallas_call(
        _ffn_kernel,
        grid=(e, cap // tm),
        in_specs=[pl.BlockSpec((1, tm, d), lambda ei, i: (ei, i, 0)),
                  pl.BlockSpec((1, d, ff), lambda ei, i: (ei, 0, 0)),
                  pl.BlockSpec((1, d, ff), lambda ei, i: (ei, 0, 0)),
                  pl.BlockSpec((1, ff, d), lambda ei, i: (ei, 0, 0)),
                  pl.BlockSpec((1, tm, 1), lambda ei, i: (ei, i, 0))],
        out_specs=pl.BlockSpec((1, tm, d), lambda ei, i: (ei, i, 0)),
        out_shape=jax.ShapeDtypeStruct((e, cap, d), F32),
        compiler_params=_cparams(("parallel", "arbitrary")),
        name="expert_ffn",
    )(xg, wg, wu, wd, gate)


def _block_diag(hd):
    i = jnp.arange(MXU_DIM)
    return (i[:, None] // hd == i[None, :] // hd).astype(BF16)


def _rope_tables_std(seq, dim):
    half = dim // 2
    inv_freq = ROPE_THETA ** (-jnp.arange(half, dtype=F32) * 2.0 / dim)
    ang = jnp.arange(seq).astype(F32)[:, None] * inv_freq[None, :]
    cos, sin = jnp.cos(ang), jnp.sin(ang)
    reps = LANES // dim
    return (jnp.tile(jnp.concatenate([cos, cos], axis=-1), (1, reps)),
            jnp.tile(jnp.concatenate([-sin, sin], axis=-1), (1, reps)))


def _rope_tables_axial(seq):
    sub = HEAD_DIM // 2
    half = sub // 2
    inv_freq = ROPE_THETA ** (-jnp.arange(half, dtype=F32) * 2.0 / sub)
    n_rows = seq // GRID_W
    rows = jnp.repeat(jnp.arange(n_rows), GRID_W).astype(F32)
    cols = jnp.tile(jnp.arange(GRID_W), n_rows).astype(F32)
    ar, ac = rows[:, None] * inv_freq[None, :], cols[:, None] * inv_freq[None, :]
    cos = jnp.concatenate([jnp.cos(ar), jnp.cos(ar), jnp.cos(ac), jnp.cos(ac)], axis=-1)
    sin = jnp.concatenate([-jnp.sin(ar), jnp.sin(ar), -jnp.sin(ac), jnp.sin(ac)], axis=-1)
    return jnp.tile(cos, (1, 2)), jnp.tile(sin, (1, 2))


def _dilated_bias(band):
    t = ATT_BLK
    rel = jnp.arange(-band - 1, band + 2)[:, None, None] * t
    d = rel + jnp.arange(t)[None, :, None] - jnp.arange(t)[None, None, :]
    cnt = jnp.zeros(d.shape, F32)
    for window, dil in DILATED_BRANCHES:
        cnt = cnt + ((d % dil == 0) & (jnp.abs(d) <= window // 2)).astype(F32)
    return jnp.where(cnt > 0, jnp.log2(jnp.maximum(cnt, 1.0)), NEG_BIG)


def _tile_gain(g, reps):
    return jnp.tile(g.astype(F32), reps)[None, :]


def _pick(n, pref):
    t = min(n, pref)
    while n % t:
        t //= 2
    return t


def _mixer_a(x, b, s, g, w_qkv, q_g, k_g):
    nq = A_HEADS * HEAD_DIM
    nkv = A_KV_HEADS * HEAD_DIM
    wq, wk, wv = w_qkv[:, :nq], w_qkv[:, nq:nq + nkv], w_qkv[:, nq + nkv:]
    wk2 = jnp.repeat(wk.reshape(D_MODEL, A_KV_HEADS, 1, HEAD_DIM), 2, axis=2).reshape(D_MODEL, -1)
    w = jnp.concatenate([wq, wk2], axis=1).astype(BF16)
    nk = 2 * nkv
    gain = jnp.concatenate([_tile_gain(q_g, nq // HEAD_DIM), _tile_gain(k_g, nk // HEAD_DIM)], axis=1)
    cos, sin = _rope_tables_axial(s)
    q, k, vt = _qkv_proj(x, g, w, wv.T.astype(BF16), gain, cos, sin, _block_diag(HEAD_DIM), nq=nq, nk=nk,
                         half=HEAD_DIM // 4, q_scale=HEAD_DIM ** -0.5 * LOG2E, v_heads=A_KV_HEADS, dv=HEAD_DIM,
                         seq=s, tm=_pick(s, 512))
    group_pairs = (A_HEADS // A_KV_HEADS) // 2
    return _attention(q.reshape(b, s, -1), k.reshape(b, s, -1), vt, mode="pair", qw=LANES, dv=HEAD_DIM, nv=1,
                      kmap=lambda p: p // group_pairs, vmap=lambda p: p // group_pairs, tq=_pick(s, ATT_TQ))


def _mixer_b(x, b, s, g, w_down, q_lat_g, kv_lat_g, w_uq, w_ukv, q_nope_g, q_rope_g, k_nope_g, k_rope_g):
    zpad = lambda a, n: jnp.concatenate([a, jnp.zeros(a.shape[:-1] + (n,), a.dtype)], axis=-1)
    w_kr = w_down[:, B_Q_LORA + B_KV_LORA:]
    wd = jnp.concatenate([w_down[:, :B_Q_LORA + B_KV_LORA], zpad(jnp.concatenate([w_kr, w_kr], axis=1), 2 * B_ROPE)],
                         axis=1).astype(BF16)
    uq = w_uq.reshape(B_Q_LORA, B_HEADS, B_NOPE + B_ROPE)
    uq_n = uq[:, :, :B_NOPE].reshape(B_Q_LORA, -1)
    uq_r = zpad(uq[:, :, B_NOPE:].reshape(B_Q_LORA, B_HEADS // 2, 2 * B_ROPE), LANES - 2 * B_ROPE).reshape(B_Q_LORA, -1)
    wuq = jnp.concatenate([uq_n, uq_r], axis=1).astype(BF16)
    ukv = w_ukv.reshape(B_KV_LORA, B_HEADS, B_NOPE + B_V)
    wuk = ukv[:, :, :B_NOPE].reshape(B_KV_LORA, -1).astype(BF16)
    wuvt = ukv[:, :, B_NOPE:].reshape(B_KV_LORA, -1).T.astype(BF16)
    gq = jnp.concatenate([_tile_gain(q_nope_g, B_HEADS), _tile_gain(q_rope_g, B_HEADS * B_NOPE // B_ROPE)], axis=1)
    gk = jnp.concatenate([_tile_gain(k_nope_g, B_HEADS), _tile_gain(k_rope_g, LANES // B_ROPE)], axis=1)
    cos, sin = _rope_tables_std(s, B_ROPE)
    q, k, vt = _mla_proj(x, g, wd, q_lat_g.astype(F32)[None, :], kv_lat_g.astype(F32)[None, :], wuq, wuk, wuvt,
                         _block_diag(B_NOPE), _block_diag(B_ROPE), gq, gk, cos, sin,
                         scale=(B_NOPE + B_ROPE) ** -0.5 * LOG2E, seq=s, tm=_pick(s, 512))
    return _attention(q.reshape(b, s, -1), k.reshape(b, s, -1), vt, mode="pair", qw=2 * LANES, dv=B_V, nv=2,
                      kmap=lambda p: p, vmap=lambda p: p, tq=_pick(s, ATT_TQ))


def _mixer_c(x, b, s, g, w_qkv, q_g, k_g, lq1, lk1, lq2, lk2, subln_g, layer_idx):
    nq = 2 * C_HEADS * HEAD_DIM
    gain = jnp.concatenate([_tile_gain(q_g, nq // HEAD_DIM), _tile_gain(k_g, nq // HEAD_DIM)], axis=1)
    cos, sin = _rope_tables_std(s, HEAD_DIM)
    q, k, vt = _qkv_proj(x, g, w_qkv[:, :2 * nq].astype(BF16), w_qkv[:, 2 * nq:].T.astype(BF16), gain, cos, sin,
                         _block_diag(HEAD_DIM), nq=nq, nk=nq, half=HEAD_DIM // 2, q_scale=HEAD_DIM ** -0.5 * LOG2E,
                         v_heads=C_HEADS, dv=2 * HEAD_DIM, seq=s, tm=_pick(s, 512))
    lam_init = LAMBDA_INIT_BASE - LAMBDA_INIT_AMP * math.exp(-LAMBDA_INIT_RATE * layer_idx)
    lam = (jnp.exp(jnp.sum(lq1.astype(F32) * lk1.astype(F32))) - jnp.exp(jnp.sum(lq2.astype(F32) * lk2.astype(F32)))
           + lam_init)
    lam_arr = jnp.stack([lam, jnp.asarray(lam_init, F32)]).reshape(1, 2).astype(F32)
    return _attention(q.reshape(b, s, -1), k.reshape(b, s, -1), vt, mode="diff", qw=LANES, dv=2 * HEAD_DIM, nv=1,
                      kmap=lambda p: p, vmap=lambda p: p, tq=_pick(s, ATT_TQ), lam=lam_arr,
                      subg=subln_g.astype(F32)[None, :])


def _mixer_d(x, b, s, g, w_qkv, q_g, k_g):
    nq = D_HEADS * HEAD_DIM
    gain = jnp.concatenate([_tile_gain(q_g, D_HEADS), _tile_gain(k_g, D_HEADS)], axis=1)
    cos, sin = _rope_tables_std(s, HEAD_DIM)
    q, k, vt = _qkv_proj(x, g, w_qkv[:, :2 * nq].astype(BF16), w_qkv[:, 2 * nq:].T.astype(BF16), gain, cos, sin,
                         _block_diag(HEAD_DIM), nq=nq, nk=nq, half=HEAD_DIM // 2, q_scale=HEAD_DIM ** -0.5 * LOG2E,
                         v_heads=D_HEADS, dv=HEAD_DIM, seq=s, tm=_pick(s, 512))
    reach = max(w // 2 for w, _ in DILATED_BRANCHES)
    band = -(-reach // ATT_BLK)
    return _attention(q.reshape(b, s, -1), k.reshape(b, s, -1), vt, mode="pair", qw=LANES, dv=HEAD_DIM, nv=2,
                      kmap=lambda p: p, vmap=lambda p: p, tq=_pick(s, 2 * ATT_BLK), bias=_dilated_bias(band), band=band)


def _ec_ffn(xnew, xn, aff, wg, wu, wd):
    n, d = xnew.shape
    cap = (EC_CAPACITY * n) // N_EXPERTS
    gate, idx = lax.top_k(aff, cap)
    xg = xn[idx]
    y = _expert_ffn(xg, wg, wu, wd, gate[..., None], tm=_pick(cap, 512))
    return xnew.at[idx.reshape(-1)].add(y.reshape(-1, d))


def kernel(x_prompt, x_sample, norm_mix_g, norm_ffn_g, a_w_qkv, a_q_norm_g, a_k_norm_g, a_w_o, b_w_down, b_q_lat_norm_g, b_kv_lat_norm_g, b_w_uq, b_w_ukv, b_q_nope_norm_g, b_q_rope_norm_g, b_k_nope_norm_g, b_k_rope_norm_g, b_w_o, c_w_qkv, c_q_norm_g, c_k_norm_g, c_lambda_q1, c_lambda_k1, c_lambda_q2, c_lambda_k2, c_subln_g, c_w_o, d_w_qkv, d_q_norm_g, d_k_norm_g, d_w_o, ec_w_router, ec_w_gate, ec_w_up, ec_w_down):
    depth = norm_mix_g.shape[0]
    wg_all, wu_all, wd_all = ec_w_gate.astype(BF16), ec_w_up.astype(BF16), ec_w_down.astype(BF16)

    def trunk(x3):
        b, s, d = x3.shape
        x = x3.reshape(b * s, d)
        for i in range(depth):
            m, j = i % N_MIXERS, i // N_MIXERS
            g = norm_mix_g[i].astype(F32)[None, :]
            if m == 0:
                o, w_o = _mixer_a(x, b, s, g, a_w_qkv[j], a_q_norm_g[j], a_k_norm_g[j]), a_w_o[j]
            elif m == 1:
                o = _mixer_b(x, b, s, g, b_w_down[j], b_q_lat_norm_g[j], b_kv_lat_norm_g[j], b_w_uq[j], b_w_ukv[j],
                             b_q_nope_norm_g[j], b_q_rope_norm_g[j], b_k_nope_norm_g[j], b_k_rope_norm_g[j])
                w_o = b_w_o[j]
            elif m == 2:
                o = _mixer_c(x, b, s, g, c_w_qkv[j], c_q_norm_g[j], c_k_norm_g[j], c_lambda_q1[j], c_lambda_k1[j],
                             c_lambda_q2[j], c_lambda_k2[j], c_subln_g[j], i)
                w_o = c_w_o[j]
            else:
                o, w_o = _mixer_d(x, b, s, g, d_w_qkv[j], d_q_norm_g[j], d_k_norm_g[j]), d_w_o[j]
            xnew, xn, aff = _oproj(o.reshape(b * s, -1), w_o.astype(BF16), x, norm_ffn_g[i].astype(F32)[None, :],
                                   ec_w_router[i].T.astype(BF16), tm=_pick(b * s, 512))
            x = _ec_ffn(xnew, xn, aff, wg_all[i], wu_all[i], wd_all[i])
        return x.reshape(b, s, d)

    return (trunk(x_prompt), trunk(x_sample))
```

```python
import functools
import math

import jax
import jax.numpy as jnp
from jax import lax
from jax.experimental import pallas as pl
from jax.experimental.pallas import tpu as pltpu

F32 = jnp.float32
BF16 = jnp.bfloat16

D_MODEL = 1024
HEAD_DIM = 64
GRID_W = 64
ROPE_THETA = 10000.0
RMS_EPS = 1e-6
N_MIXERS = 4
A_HEADS, A_KV_HEADS = 16, 4
B_HEADS, B_Q_LORA, B_KV_LORA, B_NOPE, B_ROPE, B_V = 16, 384, 256, 64, 32, 64
C_HEADS = 8
LAMBDA_INIT_BASE, LAMBDA_INIT_AMP, LAMBDA_INIT_RATE = 0.8, 0.6, 0.3
D_HEADS = 16
DILATED_BRANCHES = ((128, 1), (512, 4), (2048, 16))
N_EXPERTS = 16
EC_CAPACITY = 2
EXPERT_FF = 2 * D_MODEL

LANES = 128
MXU_DIM = 256
VMEM_LIMIT = 56 * 1024 * 1024
V_ONES = 16
ATT_BLK = MXU_DIM
ATT_TQ = 4 * ATT_BLK
SEG_TOK = MXU_DIM
SEG_ROWS_LOG2 = 8
SEG_ROWS = 1 << SEG_ROWS_LOG2
SEG_STEP_TOK = 8 * SEG_TOK
NEG_BIG = -1e30
LOG2E = math.log2(math.e)


def _cparams(sem):
    return pltpu.CompilerParams(dimension_semantics=sem, vmem_limit_bytes=VMEM_LIMIT)


_NT = (((1,), (1,)), ((), ()))


def _rms_rows(x, g):
    return x * lax.rsqrt(jnp.mean(x * x, axis=-1, keepdims=True) + RMS_EPS) * g


def _head_rms(t, bd, gain, hd):
    outs = []
    for j in range(t.shape[1] // MXU_DIM):
        tj = t[:, j * MXU_DIM:(j + 1) * MXU_DIM]
        ss = jnp.dot((tj * tj).astype(BF16), bd, preferred_element_type=F32)
        outs.append(tj * lax.rsqrt(ss * (1.0 / hd) + RMS_EPS))
    y = outs[0] if len(outs) == 1 else jnp.concatenate(outs, axis=-1)
    return y * gain


def _rope_slabs(y, cos, sin, half):
    lane = lax.broadcasted_iota(jnp.int32, (1, LANES), 1)
    first = (lane % (2 * half)) < half
    outs = []
    for j in range(y.shape[1] // LANES):
        yj = y[:, j * LANES:(j + 1) * LANES]
        swapped = jnp.where(first, pltpu.roll(yj, LANES - half, 1), pltpu.roll(yj, half, 1))
        outs.append(yj * cos + swapped * sin)
    return outs[0] if len(outs) == 1 else jnp.concatenate(outs, axis=-1)


def _store_vt(vt_ref, vt, *, heads, dv):
    hv = dv + V_ONES
    ones = jnp.ones((V_ONES, ATT_BLK), BF16)
    vt = vt.astype(BF16)
    for j in range(vt.shape[1] // ATT_BLK):
        for h in range(heads):
            vt_ref[0, j, h * hv:h * hv + dv, :] = vt[h * dv:(h + 1) * dv, j * ATT_BLK:(j + 1) * ATT_BLK]
            vt_ref[0, j, h * hv + dv:(h + 1) * hv, :] = ones


def _qkv_kernel(x_ref, g_ref, w_ref, wvt_ref, bd_ref, gain_ref, cos_ref, sin_ref, q_ref, k_ref, vt_ref,
                *, nq, half, q_scale, v_heads, dv):
    xn = _rms_rows(x_ref[...], g_ref[...]).astype(BF16)
    y = jnp.dot(xn, w_ref[...], preferred_element_type=F32)
    bd = bd_ref[...]
    cos, sin = cos_ref[...], sin_ref[...]
    gain = gain_ref[...]
    q = _rope_slabs(_head_rms(y[:, :nq], bd, gain[:, :nq], HEAD_DIM), cos, sin, half) * q_scale
    k = _rope_slabs(_head_rms(y[:, nq:], bd, gain[:, nq:], HEAD_DIM), cos, sin, half)
    q_ref[...] = q.astype(BF16)
    k_ref[...] = k.astype(BF16)
    vt = lax.dot_general(wvt_ref[...], xn, _NT, preferred_element_type=F32)
    _store_vt(vt_ref, vt, heads=v_heads, dv=dv)


def _vt_out(n, seq, tm, vr):
    nt = seq // tm
    spec = pl.BlockSpec((1, tm // ATT_BLK, vr, ATT_BLK), lambda i: (i // nt, i % nt, 0, 0))
    return spec, jax.ShapeDtypeStruct((n // seq, seq // ATT_BLK, vr, ATT_BLK), BF16)


def _qkv_proj(x, g, w, wvt, gain, cos, sin, bd, *, nq, nk, half, q_scale, v_heads, dv, seq, tm):
    n, d = x.shape
    nt = seq // tm
    row = lambda i: (i, 0)
    const = lambda i: (0, 0)
    vt_spec, vt_shape = _vt_out(n, seq, tm, v_heads * (dv + V_ONES))
    return pl.pallas_call(
        functools.partial(_qkv_kernel, nq=nq, half=half, q_scale=q_scale, v_heads=v_heads, dv=dv),
        grid=(n // tm,),
        in_specs=[pl.BlockSpec((tm, d), row), pl.BlockSpec((1, d), const),
                  pl.BlockSpec(w.shape, const), pl.BlockSpec(wvt.shape, const), pl.BlockSpec(bd.shape, const),
                  pl.BlockSpec(gain.shape, const),
                  pl.BlockSpec((tm, LANES), lambda i: (i % nt, 0)),
                  pl.BlockSpec((tm, LANES), lambda i: (i % nt, 0))],
        out_specs=[pl.BlockSpec((tm, nq), row), pl.BlockSpec((tm, nk), row), vt_spec],
        out_shape=[jax.ShapeDtypeStruct((n, nq), BF16), jax.ShapeDtypeStruct((n, nk), BF16), vt_shape],
        compiler_params=_cparams(("parallel",)),
        name="qkv_proj",
    )(x, g, w, wvt, bd, gain, cos, sin)


def _mla_kernel(x_ref, g_ref, wd_ref, qlg_ref, kvlg_ref, wuq_ref, wuk_ref, wuvt_ref, bd64_ref, bd32_ref,
                gq_ref, gk_ref, cos_ref, sin_ref, q_ref, k_ref, vt_ref, *, scale):
    xn = _rms_rows(x_ref[...], g_ref[...]).astype(BF16)
    c = jnp.dot(xn, wd_ref[...], preferred_element_type=F32)
    cq = _rms_rows(c[:, :B_Q_LORA], qlg_ref[...]).astype(BF16)
    ckv = _rms_rows(c[:, B_Q_LORA:B_Q_LORA + B_KV_LORA], kvlg_ref[...]).astype(BF16)
    q = jnp.dot(cq, wuq_ref[...], preferred_element_type=F32)
    kn = jnp.dot(ckv, wuk_ref[...], preferred_element_type=F32)
    bd64, bd32 = bd64_ref[...], bd32_ref[...]
    cos, sin = cos_ref[...], sin_ref[...]
    gq, gk = gq_ref[...], gk_ref[...]
    hw = B_HEADS * B_NOPE
    qn = _head_rms(q[:, :hw], bd64, gq[:, :hw], B_NOPE) * scale
    qr = _rope_slabs(_head_rms(q[:, hw:], bd32, gq[:, hw:], B_ROPE), cos, sin, B_ROPE // 2) * scale
    kn = _head_rms(kn, bd64, gk[:, :hw], B_NOPE)
    kr_raw = c[:, B_Q_LORA + B_KV_LORA:]
    kr2 = _head_rms(jnp.concatenate([kr_raw, kr_raw], axis=-1), bd32,
                    jnp.concatenate([gk[:, hw:], gk[:, hw:]], axis=-1), B_ROPE)
    kr = _rope_slabs(kr2[:, :LANES], cos, sin, B_ROPE // 2).astype(BF16)
    qn, qr, kn = qn.astype(BF16), qr.astype(BF16), kn.astype(BF16)
    for p in range(B_HEADS // 2):
        sl = slice(p * LANES, (p + 1) * LANES)
        q_ref[:, 2 * p * LANES:(2 * p + 1) * LANES] = qn[:, sl]
        q_ref[:, (2 * p + 1) * LANES:(2 * p + 2) * LANES] = qr[:, sl]
        k_ref[:, 2 * p * LANES:(2 * p + 1) * LANES] = kn[:, sl]
        k_ref[:, (2 * p + 1) * LANES:(2 * p + 2) * LANES] = kr
    vt = lax.dot_general(wuvt_ref[...], ckv, _NT, preferred_element_type=F32)
    _store_vt(vt_ref, vt, heads=B_HEADS, dv=B_V)


def _mla_proj(x, g, wd, qlg, kvlg, wuq, wuk, wuvt, bd64, bd32, gq, gk, cos, sin, *, scale, seq, tm):
    n, d = x.shape
    nt = seq // tm
    row = lambda i: (i, 0)
    const = lambda i: (0, 0)
    full = lambda a: pl.BlockSpec(a.shape, const)
    wq = (B_HEADS // 2) * 2 * LANES
    vt_spec, vt_shape = _vt_out(n, seq, tm, B_HEADS * (B_V + V_ONES))
    return pl.pallas_call(
        functools.partial(_mla_kernel, scale=scale),
        grid=(n // tm,),
        in_specs=[pl.BlockSpec((tm, d), row), full(g), full(wd), full(qlg), full(kvlg), full(wuq), full(wuk),
                  full(wuvt), full(bd64), full(bd32), full(gq), full(gk),
                  pl.BlockSpec((tm, LANES), lambda i: (i % nt, 0)),
                  pl.BlockSpec((tm, LANES), lambda i: (i % nt, 0))],
        out_specs=[pl.BlockSpec((tm, wq), row), pl.BlockSpec((tm, wq), row), vt_spec],
        out_shape=[jax.ShapeDtypeStruct((n, wq), BF16), jax.ShapeDtypeStruct((n, wq), BF16), vt_shape],
        compiler_params=_cparams(("parallel",)),
        name="mla_proj",
    )(x, g, wd, qlg, kvlg, wuq, wuk, wuvt, bd64, bd32, gq, gk, cos, sin)


def _attn_kernel(*refs, mode, qw, dv, nv, tq, nk, band):
    it = iter(refs)
    q_ref, k_ref, vt_ref = next(it), next(it), next(it)
    bias_ref = next(it) if band is not None else None
    if mode == "diff":
        lam_ref, subg_ref = next(it), next(it)
    o_ref = next(it)
    qm_sc, m_sc, acc_sc, sa_sc, sb_sc = next(it), next(it), next(it), next(it), next(it)

    blk = ATT_BLK
    hv = dv + V_ONES
    nsub = tq // blk
    chains = [(r, h) for r in range(nsub) for h in range(2)]
    qi = pl.program_id(2)

    lane = lax.broadcasted_iota(jnp.int32, (1, qw), 1)
    if qw == LANES:
        masks = (lane < HEAD_DIM, lane >= HEAD_DIM)
    else:
        masks = ((lane < HEAD_DIM) | ((lane >= LANES) & (lane < LANES + B_ROPE)),
                 ((lane >= HEAD_DIM) & (lane < LANES)) | ((lane >= LANES + B_ROPE) & (lane < LANES + 2 * B_ROPE)))
    for ci, (r, h) in enumerate(chains):
        qr = q_ref[0, r * blk:(r + 1) * blk, :]
        qm_sc[ci] = jnp.where(masks[h], qr, jnp.zeros_like(qr))
    m_sc[...] = jnp.full(m_sc.shape, NEG_BIG, F32)
    acc_sc[...] = jnp.zeros(acc_sc.shape, F32)

    def scores(c, s_sc):
        start = pl.multiple_of(c * blk, blk)
        k = k_ref[0, pl.ds(start, blk), :]
        for ci in range(len(chains)):
            st = lax.dot_general(k, qm_sc[ci], _NT, preferred_element_type=F32)
            if band is not None:
                rel = c - (qi * nsub + chains[ci][0]) + band + 1
                st = st + bias_ref[jnp.clip(rel, 0, 2 * band + 2)]
            s_sc[ci] = st

    def consume(c, s_sc):
        vt = vt_ref[0, c]
        ps, alphas = [], []
        for ci in range(len(chains)):
            st = s_sc[ci]
            m_old = m_sc[ci]
            m_new = jnp.maximum(m_old, jnp.max(st, axis=0, keepdims=True))
            ps.append(jnp.exp2(st - m_new).astype(BF16))
            alphas.append(jnp.exp2(m_old - m_new))
            m_sc[ci] = m_new
        for ci, (r, h) in enumerate(chains):
            voff = (h if nv == 2 else 0) * hv
            pv = jnp.dot(vt[voff:voff + hv], ps[ci], preferred_element_type=F32)
            acc_sc[ci] = acc_sc[ci] * alphas[ci] + pv

    if band is None:
        lo, hi = 0, nk
    else:
        lo, hi = jnp.maximum(qi * nsub - band, 0), jnp.minimum((qi + 1) * nsub + band, nk)
    npairs = (hi - lo - 1) // 2

    def body(i, carry):
        c = lo + 2 * i
        scores(c + 1, sb_sc)
        consume(c, sa_sc)
        scores(c + 2, sa_sc)
        consume(c + 1, sb_sc)
        return carry

    scores(lo, sa_sc)
    lax.fori_loop(0, npairs, body, 0)
    c_tail = lo + 2 * npairs
    if band is None:
        if (hi - lo) % 2 == 0:
            scores(c_tail + 1, sb_sc)
            consume(c_tail, sa_sc)
            consume(c_tail + 1, sb_sc)
        else:
            consume(c_tail, sa_sc)
    else:
        two_left = (hi - c_tail) == 2

        @pl.when(two_left)
        def _():
            scores(c_tail + 1, sb_sc)

        consume(c_tail, sa_sc)

        @pl.when(two_left)
        def _():
            consume(c_tail + 1, sb_sc)

    for r in range(nsub):
        a, b = acc_sc[2 * r], acc_sc[2 * r + 1]
        oa = a[:dv] / a[dv:dv + 1]
        ob = b[:dv] / b[dv:dv + 1]
        if mode == "pair":
            o = jnp.concatenate([oa, ob], axis=0).T
        else:
            o = (oa - lam_ref[0, 0] * ob).T
            o = _rms_rows(o, subg_ref[...]) * (1.0 - lam_ref[0, 1])
        o_ref[0, r * blk:(r + 1) * blk, :] = o.astype(o_ref.dtype)


def _attention(q, k, vt, *, mode, qw, dv, nv, kmap, vmap, tq, bias=None, band=None, lam=None, subg=None):
    b, s, cq = q.shape
    pairs = cq // qw
    nk = s // ATT_BLK
    hv = dv + V_ONES
    in_specs = [pl.BlockSpec((1, tq, qw), lambda bi, p, i: (bi, i, p)),
                pl.BlockSpec((1, s, qw), lambda bi, p, i: (bi, 0, kmap(p))),
                pl.BlockSpec((1, nk, nv * hv, ATT_BLK), lambda bi, p, i: (bi, 0, vmap(p), 0))]
    args = [q, k, vt]
    if band is not None:
        in_specs.append(pl.BlockSpec(bias.shape, lambda bi, p, i: (0, 0, 0)))
        args.append(bias)
    if mode == "diff":
        in_specs.append(pl.BlockSpec(memory_space=pltpu.SMEM))
        in_specs.append(pl.BlockSpec(subg.shape, lambda bi, p, i: (0, 0)))
        args += [lam, subg]
    nchains = 2 * (tq // ATT_BLK)
    return pl.pallas_call(
        functools.partial(_attn_kernel, mode=mode, qw=qw, dv=dv, nv=nv, tq=tq, nk=nk, band=band),
        grid=(b, pairs, s // tq),
        in_specs=in_specs,
        out_specs=pl.BlockSpec((1, tq, LANES), lambda bi, p, i: (bi, i, p)),
        out_shape=jax.ShapeDtypeStruct((b, s, pairs * LANES), BF16),
        scratch_shapes=[pltpu.VMEM((nchains, ATT_BLK, qw), BF16), pltpu.VMEM((nchains, 1, ATT_BLK), F32),
                        pltpu.VMEM((nchains, hv, ATT_BLK), F32),
                        pltpu.VMEM((nchains, ATT_BLK, ATT_BLK), F32), pltpu.VMEM((nchains, ATT_BLK, ATT_BLK), F32)],
        compiler_params=_cparams(("parallel", "parallel", "arbitrary")),
        name="attention_" + mode,
    )(*args)


def _oproj_kernel(o_ref, w_ref, x_ref, g_ref, wr_ref, xnew_ref, xn_ref, aff_ref):
    xnew = x_ref[...] + jnp.dot(o_ref[...], w_ref[...], preferred_element_type=F32)
    xnew_ref[...] = xnew
    xn = _rms_rows(xnew, g_ref[...])
    xn_ref[...] = xn
    logits = lax.dot_general(wr_ref[...], xn.astype(BF16), _NT, preferred_element_type=F32)
    z = logits - jnp.max(logits, axis=0, keepdims=True)
    e = jnp.exp(z)
    aff_ref[...] = e / jnp.sum(e, axis=0, keepdims=True)


def _oproj(o, w, x, g, wr_t, *, tm):
    n, d = x.shape
    row = lambda i: (i, 0)
    const = lambda i: (0, 0)
    return pl.pallas_call(
        _oproj_kernel,
        grid=(n // tm,),
        in_specs=[pl.BlockSpec((tm, o.shape[1]), row), pl.BlockSpec(w.shape, const), pl.BlockSpec((tm, d), row),
                  pl.BlockSpec((1, d), const), pl.BlockSpec(wr_t.shape, const)],
        out_specs=[pl.BlockSpec((tm, d), row), pl.BlockSpec((tm, d), row),
                   pl.BlockSpec((N_EXPERTS, tm), lambda i: (0, i))],
        out_shape=[jax.ShapeDtypeStruct((n, d), F32), jax.ShapeDtypeStruct((n, d), F32),
                   jax.ShapeDtypeStruct((N_EXPERTS, n), F32)],
        compiler_params=_cparams(("parallel",)),
        name="oproj_router",
    )(o, w, x, g, wr_t)


def _ffn_kernel(xg_ref, wg_ref, wu_ref, wd_ref, gate_ref, y_ref):
    xg = xg_ref[0].astype(BF16)
    a = jnp.dot(xg, wg_ref[0], preferred_element_type=F32)
    u = jnp.dot(xg, wu_ref[0], preferred_element_type=F32)
    h = (a * jax.nn.sigmoid(a) * u).astype(BF16)
    y_ref[0] = jnp.dot(h, wd_ref[0], preferred_element_type=F32) * gate_ref[0]


def _expert_ffn(xg, wg, wu, wd, gate, *, tm):
    e, cap, d = xg.shape
    ff = wg.shape[2]
    return pl.pallas_call(
        _ffn_kernel,
        grid=(e, cap // tm),
        in_specs=[pl.BlockSpec((1, tm, d), lambda ei, i: (ei, i, 0)),
                  pl.BlockSpec((1, d, ff), lambda ei, i: (ei, 0, 0)),
                  pl.BlockSpec((1, d, ff), lambda ei, i: (ei, 0, 0)),
                  pl.BlockSpec((1, ff, d), lambda ei, i: (ei, 0, 0)),
                  pl.BlockSpec((1, tm, 1), lambda ei, i: (ei, i, 0))],
        out_specs=pl.BlockSpec((1, tm, d), lambda ei, i: (ei, i, 0)),
        out_shape=jax.ShapeDtypeStruct((e, cap, d), F32),
        compiler_params=_cparams(("parallel", "arbitrary")),
        name="expert_ffn",
    )(xg, wg, wu, wd, gate)


def _combine_kernel(offs_ref, x_ref, tok_hbm, z_hbm, o_ref, tokbuf, zbuf, sem, *, nsub):
    step = pl.program_id(0)
    blk0 = step * nsub
    c_lo = lax.shift_right_logical(offs_ref[blk0], SEG_ROWS_LOG2)
    c_hi = lax.shift_right_logical(offs_ref[blk0 + nsub] + (SEG_ROWS - 1), SEG_ROWS_LOG2)
    o_ref[...] = x_ref[...]

    def copies(c, slot):
        return (pltpu.make_async_copy(tok_hbm.at[c], tokbuf.at[slot], sem.at[0, slot]),
                pltpu.make_async_copy(z_hbm.at[pl.ds(pl.multiple_of(c * SEG_ROWS, SEG_ROWS), SEG_ROWS)],
                                      zbuf.at[slot], sem.at[1, slot]))

    @pl.when(c_hi > c_lo)
    def _():
        for cp in copies(c_lo, 0):
            cp.start()

    def body(c, carry):
        slot = jnp.bitwise_and(c - c_lo, 1)
        for cp in copies(c, slot):
            cp.wait()

        @pl.when(c + 1 < c_hi)
        def _():
            for cp in copies(c + 1, 1 - slot):
                cp.start()

        tok = tokbuf[slot]
        z = zbuf[slot]
        zh = z.astype(BF16)
        zl = (z - zh.astype(F32)).astype(BF16)
        row0 = c * SEG_ROWS
        for u in range(nsub):
            lo_u, hi_u = offs_ref[blk0 + u], offs_ref[blk0 + u + 1]

            @pl.when((lo_u < row0 + SEG_ROWS) & (hi_u > row0))
            def _():
                tid = (blk0 + u) * SEG_TOK + lax.broadcasted_iota(jnp.int32, (SEG_TOK, SEG_ROWS), 0)
                onehot = jnp.where(tok == tid, 1.0, 0.0).astype(BF16)
                upd = (jnp.dot(onehot, zh, preferred_element_type=F32)
                       + jnp.dot(onehot, zl, preferred_element_type=F32))
                o_ref[u * SEG_TOK:(u + 1) * SEG_TOK, :] += upd
        return carry

    lax.fori_loop(c_lo, c_hi, body, 0)


def _combine(offs, x, tok, z):
    n, d = x.shape
    tn = _pick(n, SEG_STEP_TOK)
    return pl.pallas_call(
        functools.partial(_combine_kernel, nsub=tn // SEG_TOK),
        grid_spec=pltpu.PrefetchScalarGridSpec(
            num_scalar_prefetch=1,
            grid=(n // tn,),
            in_specs=[pl.BlockSpec((tn, d), lambda i, offs: (i, 0)),
                      pl.BlockSpec(memory_space=pl.ANY), pl.BlockSpec(memory_space=pl.ANY)],
            out_specs=pl.BlockSpec((tn, d), lambda i, offs: (i, 0)),
            scratch_shapes=[pltpu.VMEM((2, 1, SEG_ROWS), jnp.int32), pltpu.VMEM((2, SEG_ROWS, d), F32),
                            pltpu.SemaphoreType.DMA((2, 2))]),
        out_shape=jax.ShapeDtypeStruct((n, d), F32),
        compiler_params=_cparams(("arbitrary",)),
        name="combine_segsum",
    )(offs, x, tok, z)


def _block_diag(hd):
    i = jnp.arange(MXU_DIM)
    return (i[:, None] // hd == i[None, :] // hd).astype(BF16)


def _rope_tables_std(seq, dim):
    half = dim // 2
    inv_freq = ROPE_THETA ** (-jnp.arange(half, dtype=F32) * 2.0 / dim)
    ang = jnp.arange(seq).astype(F32)[:, None] * inv_freq[None, :]
    cos, sin = jnp.cos(ang), jnp.sin(ang)
    reps = LANES // dim
    return (jnp.tile(jnp.concatenate([cos, cos], axis=-1), (1, reps)),
            jnp.tile(jnp.concatenate([-sin, sin], axis=-1), (1, reps)))


def _rope_tables_axial(seq):
    sub = HEAD_DIM // 2
    half = sub // 2
    inv_freq = ROPE_THETA ** (-jnp.arange(half, dtype=F32) * 2.0 / sub)
    n_rows = seq // GRID_W
    rows = jnp.repeat(jnp.arange(n_rows), GRID_W).astype(F32)
    cols = jnp.tile(jnp.arange(GRID_W), n_rows).astype(F32)
    ar, ac = rows[:, None] * inv_freq[None, :], cols[:, None] * inv_freq[None, :]
    cos = jnp.concatenate([jnp.cos(ar), jnp.cos(ar), jnp.cos(ac), jnp.cos(ac)], axis=-1)
    sin = jnp.concatenate([-jnp.sin(ar), jnp.sin(ar), -jnp.sin(ac), jnp.sin(ac)], axis=-1)
    return jnp.tile(cos, (1, 2)), jnp.tile(sin, (1, 2))


def _dilated_bias(band):
    t = ATT_BLK
    rel = jnp.arange(-band - 1, band + 2)[:, None, None] * t
    d = rel + jnp.arange(t)[None, :, None] - jnp.arange(t)[None, None, :]
    cnt = jnp.zeros(d.shape, F32)
    for window, dil in DILATED_BRANCHES:
        cnt = cnt + ((d % dil == 0) & (jnp.abs(d) <= window // 2)).astype(F32)
    return jnp.where(cnt > 0, jnp.log2(jnp.maximum(cnt, 1.0)), NEG_BIG)


def _tile_gain(g, reps):
    return jnp.tile(g.astype(F32), reps)[None, :]


def _pick(n, pref):
    t = min(n, pref)
    while n % t:
        t //= 2
    return t


def _mixer_a(x, b, s, g, w_qkv, q_g, k_g):
    nq = A_HEADS * HEAD_DIM
    nkv = A_KV_HEADS * HEAD_DIM
    wq, wk, wv = w_qkv[:, :nq], w_qkv[:, nq:nq + nkv], w_qkv[:, nq + nkv:]
    wk2 = jnp.repeat(wk.reshape(D_MODEL, A_KV_HEADS, 1, HEAD_DIM), 2, axis=2).reshape(D_MODEL, -1)
    w = jnp.concatenate([wq, wk2], axis=1).astype(BF16)
    nk = 2 * nkv
    gain = jnp.concatenate([_tile_gain(q_g, nq // HEAD_DIM), _tile_gain(k_g, nk // HEAD_DIM)], axis=1)
    cos, sin = _rope_tables_axial(s)
    q, k, vt = _qkv_proj(x, g, w, wv.T.astype(BF16), gain, cos, sin, _block_diag(HEAD_DIM), nq=nq, nk=nk,
                         half=HEAD_DIM // 4, q_scale=HEAD_DIM ** -0.5 * LOG2E, v_heads=A_KV_HEADS, dv=HEAD_DIM,
                         seq=s, tm=_pick(s, 512))
    group_pairs = (A_HEADS // A_KV_HEADS) // 2
    return _attention(q.reshape(b, s, -1), k.reshape(b, s, -1), vt, mode="pair", qw=LANES, dv=HEAD_DIM, nv=1,
                      kmap=lambda p: p // group_pairs, vmap=lambda p: p // group_pairs, tq=_pick(s, ATT_TQ))


def _mixer_b(x, b, s, g, w_down, q_lat_g, kv_lat_g, w_uq, w_ukv, q_nope_g, q_rope_g, k_nope_g, k_rope_g):
    zpad = lambda a, n: jnp.concatenate([a, jnp.zeros(a.shape[:-1] + (n,), a.dtype)], axis=-1)
    w_kr = w_down[:, B_Q_LORA + B_KV_LORA:]
    wd = jnp.concatenate([w_down[:, :B_Q_LORA + B_KV_LORA], zpad(jnp.concatenate([w_kr, w_kr], axis=1), 2 * B_ROPE)],
                         axis=1).astype(BF16)
    uq = w_uq.reshape(B_Q_LORA, B_HEADS, B_NOPE + B_ROPE)
    uq_n = uq[:, :, :B_NOPE].reshape(B_Q_LORA, -1)
    uq_r = zpad(uq[:, :, B_NOPE:].reshape(B_Q_LORA, B_HEADS // 2, 2 * B_ROPE), LANES - 2 * B_ROPE).reshape(B_Q_LORA, -1)
    wuq = jnp.concatenate([uq_n, uq_r], axis=1).astype(BF16)
    ukv = w_ukv.reshape(B_KV_LORA, B_HEADS, B_NOPE + B_V)
    wuk = ukv[:, :, :B_NOPE].reshape(B_KV_LORA, -1).astype(BF16)
    wuvt = ukv[:, :, B_NOPE:].reshape(B_KV_LORA, -1).T.astype(BF16)
    gq = jnp.concatenate([_tile_gain(q_nope_g, B_HEADS), _tile_gain(q_rope_g, B_HEADS * B_NOPE // B_ROPE)], axis=1)
    gk = jnp.concatenate([_tile_gain(k_nope_g, B_HEADS), _tile_gain(k_rope_g, LANES // B_ROPE)], axis=1)
    cos, sin = _rope_tables_std(s, B_ROPE)
    q, k, vt = _mla_proj(x, g, wd, q_lat_g.astype(F32)[None, :], kv_lat_g.astype(F32)[None, :], wuq, wuk, wuvt,
                         _block_diag(B_NOPE), _block_diag(B_ROPE), gq, gk, cos, sin,
                         scale=(B_NOPE + B_ROPE) ** -0.5 * LOG2E, seq=s, tm=_pick(s, 512))
    return _attention(q.reshape(b, s, -1), k.reshape(b, s, -1), vt, mode="pair", qw=2 * LANES, dv=B_V, nv=2,
                      kmap=lambda p: p, vmap=lambda p: p, tq=_pick(s, ATT_TQ))


def _mixer_c(x, b, s, g, w_qkv, q_g, k_g, lq1, lk1, lq2, lk2, subln_g, layer_idx):
    nq = 2 * C_HEADS * HEAD_DIM
    gain = jnp.concatenate([_tile_gain(q_g, nq // HEAD_DIM), _tile_gain(k_g, nq // HEAD_DIM)], axis=1)
    cos, sin = _rope_tables_std(s, HEAD_DIM)
    q, k, vt = _qkv_proj(x, g, w_qkv[:, :2 * nq].astype(BF16), w_qkv[:, 2 * nq:].T.astype(BF16), gain, cos, sin,
                         _block_diag(HEAD_DIM), nq=nq, nk=nq, half=HEAD_DIM // 2, q_scale=HEAD_DIM ** -0.5 * LOG2E,
                         v_heads=C_HEADS, dv=2 * HEAD_DIM, seq=s, tm=_pick(s, 512))
    lam_init = LAMBDA_INIT_BASE - LAMBDA_INIT_AMP * math.exp(-LAMBDA_INIT_RATE * layer_idx)
    lam = (jnp.exp(jnp.sum(lq1.astype(F32) * lk1.astype(F32))) - jnp.exp(jnp.sum(lq2.astype(F32) * lk2.astype(F32)))
           + lam_init)
    lam_arr = jnp.stack([lam, jnp.asarray(lam_init, F32)]).reshape(1, 2).astype(F32)
    return _attention(q.reshape(b, s, -1), k.reshape(b, s, -1), vt, mode="diff", qw=LANES, dv=2 * HEAD_DIM, nv=1,
                      kmap=lambda p: p, vmap=lambda p: p, tq=_pick(s, ATT_TQ), lam=lam_arr,
                      subg=subln_g.astype(F32)[None, :])


def _mixer_d(x, b, s, g, w_qkv, q_g, k_g):
    nq = D_HEADS * HEAD_DIM
    gain = jnp.concatenate([_tile_gain(q_g, D_HEADS), _tile_gain(k_g, D_HEADS)], axis=1)
    cos, sin = _rope_tables_std(s, HEAD_DIM)
    q, k, vt = _qkv_proj(x, g, w_qkv[:, :2 * nq].astype(BF16), w_qkv[:, 2 * nq:].T.astype(BF16), gain, cos, sin,
                         _block_diag(HEAD_DIM), nq=nq, nk=nq, half=HEAD_DIM // 2, q_scale=HEAD_DIM ** -0.5 * LOG2E,
                         v_heads=D_HEADS, dv=HEAD_DIM, seq=s, tm=_pick(s, 512))
    reach = max(w // 2 for w, _ in DILATED_BRANCHES)
    band = -(-reach // ATT_BLK)
    return _attention(q.reshape(b, s, -1), k.reshape(b, s, -1), vt, mode="pair", qw=LANES, dv=HEAD_DIM, nv=2,
                      kmap=lambda p: p, vmap=lambda p: p, tq=_pick(s, 2 * ATT_BLK), bias=_dilated_bias(band), band=band)


def _ec_ffn(xnew, xn, aff, wg, wu, wd):
    n, d = xnew.shape
    cap = (EC_CAPACITY * n) // N_EXPERTS
    gate, idx = lax.top_k(aff, cap)
    y = _expert_ffn(xn[idx], wg, wu, wd, gate[..., None], tm=_pick(cap, 512))
    flat = idx.reshape(-1)
    tok_sorted, order = lax.sort_key_val(flat, jnp.arange(flat.size, dtype=jnp.int32))
    bounds = jnp.arange(0, n + 1, SEG_TOK, dtype=jnp.int32)
    offs = jnp.sum((tok_sorted[None, :] < bounds[:, None]).astype(jnp.int32), axis=1)
    return _combine(offs, xnew, tok_sorted.reshape(-1, 1, SEG_ROWS), y.reshape(-1, d)[order])


def kernel(x_prompt, x_sample, norm_mix_g, norm_ffn_g, a_w_qkv, a_q_norm_g, a_k_norm_g, a_w_o, b_w_down, b_q_lat_norm_g, b_kv_lat_norm_g, b_w_uq, b_w_ukv, b_q_nope_norm_g, b_q_rope_norm_g, b_k_nope_norm_g, b_k_rope_norm_g, b_w_o, c_w_qkv, c_q_norm_g, c_k_norm_g, c_lambda_q1, c_lambda_k1, c_lambda_q2, c_lambda_k2, c_subln_g, c_w_o, d_w_qkv, d_q_norm_g, d_k_norm_g, d_w_o, ec_w_router, ec_w_gate, ec_w_up, ec_w_down):
    depth = norm_mix_g.shape[0]
    wg_all, wu_all, wd_all = ec_w_gate.astype(BF16), ec_w_up.astype(BF16), ec_w_down.astype(BF16)

    def trunk(x3):
        b, s, d = x3.shape
        x = x3.reshape(b * s, d)
        for i in range(depth):
            m, j = i % N_MIXERS, i // N_MIXERS
            g = norm_mix_g[i].astype(F32)[None, :]
            if m == 0:
                o, w_o = _mixer_a(x, b, s, g, a_w_qkv[j], a_q_norm_g[j], a_k_norm_g[j]), a_w_o[j]
            elif m == 1:
                o = _mixer_b(x, b, s, g, b_w_down[j], b_q_lat_norm_g[j], b_kv_lat_norm_g[j], b_w_uq[j], b_w_ukv[j],
                             b_q_nope_norm_g[j], b_q_rope_norm_g[j], b_k_nope_norm_g[j], b_k_rope_norm_g[j])
                w_o = b_w_o[j]
            elif m == 2:
                o = _mixer_c(x, b, s, g, c_w_qkv[j], c_q_norm_g[j], c_k_norm_g[j], c_lambda_q1[j], c_lambda_k1[j],
                             c_lambda_q2[j], c_lambda_k2[j], c_subln_g[j], i)
                w_o = c_w_o[j]
            else:
                o, w_o = _mixer_d(x, b, s, g, d_w_qkv[j], d_q_norm_g[j], d_k_norm_g[j]), d_w_o[j]
            xnew, xn, aff = _oproj(o.reshape(b * s, -1), w_o.astype(BF16), x, norm_ffn_g[i].astype(F32)[None, :],
                                   ec_w_router[i].T.astype(BF16), tm=_pick(b * s, 512))
            x = _ec_ffn(xnew, xn, aff, wg_all[i], wu_all[i], wd_all[i])
        return x.reshape(b, s, d)

    return (trunk(x_prompt), trunk(x_sample))
```

```python
import functools
import math

import jax
import jax.numpy as jnp
from jax import lax
from jax.experimental import pallas as pl
from jax.experimental.pallas import tpu as pltpu

F32 = jnp.float32
BF16 = jnp.bfloat16

D_MODEL = 1024
HEAD_DIM = 64
GRID_W = 64
ROPE_THETA = 10000.0
RMS_EPS = 1e-6
N_MIXERS = 4
A_HEADS, A_KV_HEADS = 16, 4
B_HEADS, B_Q_LORA, B_KV_LORA, B_NOPE, B_ROPE, B_V = 16, 384, 256, 64, 32, 64
C_HEADS = 8
LAMBDA_INIT_BASE, LAMBDA_INIT_AMP, LAMBDA_INIT_RATE = 0.8, 0.6, 0.3
D_HEADS = 16
DILATED_BRANCHES = ((128, 1), (512, 4), (2048, 16))
N_EXPERTS = 16
EC_CAPACITY = 2
EXPERT_FF = 2 * D_MODEL

LANES = 128
MXU_DIM = 256
VMEM_LIMIT = 56 * 1024 * 1024
V_ONES = 16
ATT_BLK = MXU_DIM
ATT_TQ = 4 * ATT_BLK
SEG_TOK = MXU_DIM
SEG_ROWS_LOG2 = 8
SEG_ROWS = 1 << SEG_ROWS_LOG2
SEG_STEP_TOK = 8 * SEG_TOK
SEG_BUFS = 4
NEG_BIG = -1e30
LOG2E = math.log2(math.e)


def _cparams(sem):
    return pltpu.CompilerParams(dimension_semantics=sem, vmem_limit_bytes=VMEM_LIMIT)


_NT = (((1,), (1,)), ((), ()))


def _rms_rows(x, g):
    return x * lax.rsqrt(jnp.mean(x * x, axis=-1, keepdims=True) + RMS_EPS) * g


def _head_rms(t, bd, gain, hd):
    outs = []
    for j in range(t.shape[1] // MXU_DIM):
        tj = t[:, j * MXU_DIM:(j + 1) * MXU_DIM]
        ss = jnp.dot((tj * tj).astype(BF16), bd, preferred_element_type=F32)
        outs.append(tj * lax.rsqrt(ss * (1.0 / hd) + RMS_EPS))
    y = outs[0] if len(outs) == 1 else jnp.concatenate(outs, axis=-1)
    return y * gain


def _rope_slabs(y, cos, sin, half):
    lane = lax.broadcasted_iota(jnp.int32, (1, LANES), 1)
    first = (lane % (2 * half)) < half
    outs = []
    for j in range(y.shape[1] // LANES):
        yj = y[:, j * LANES:(j + 1) * LANES]
        swapped = jnp.where(first, pltpu.roll(yj, LANES - half, 1), pltpu.roll(yj, half, 1))
        outs.append(yj * cos + swapped * sin)
    return outs[0] if len(outs) == 1 else jnp.concatenate(outs, axis=-1)


def _store_vt(vt_ref, vt, *, heads, dv):
    hv = dv + V_ONES
    ones = jnp.ones((V_ONES, ATT_BLK), BF16)
    vt = vt.astype(BF16)
    for j in range(vt.shape[1] // ATT_BLK):
        for h in range(heads):
            vt_ref[0, j, h * hv:h * hv + dv, :] = vt[h * dv:(h + 1) * dv, j * ATT_BLK:(j + 1) * ATT_BLK]
            vt_ref[0, j, h * hv + dv:(h + 1) * hv, :] = ones


def _qkv_kernel(x_ref, g_ref, w_ref, wvt_ref, bd_ref, gain_ref, cos_ref, sin_ref, q_ref, k_ref, vt_ref,
                *, nq, half, q_scale, v_heads, dv):
    xn = _rms_rows(x_ref[...], g_ref[...]).astype(BF16)
    y = jnp.dot(xn, w_ref[...], preferred_element_type=F32)
    bd = bd_ref[...]
    cos, sin = cos_ref[...], sin_ref[...]
    gain = gain_ref[...]
    q = _rope_slabs(_head_rms(y[:, :nq], bd, gain[:, :nq], HEAD_DIM), cos, sin, half) * q_scale
    k = _rope_slabs(_head_rms(y[:, nq:], bd, gain[:, nq:], HEAD_DIM), cos, sin, half)
    q_ref[...] = q.astype(BF16)
    k_ref[...] = k.astype(BF16)
    vt = lax.dot_general(wvt_ref[...], xn, _NT, preferred_element_type=F32)
    _store_vt(vt_ref, vt, heads=v_heads, dv=dv)


def _vt_out(n, seq, tm, vr):
    nt = seq // tm
    spec = pl.BlockSpec((1, tm // ATT_BLK, vr, ATT_BLK), lambda i: (i // nt, i % nt, 0, 0))
    return spec, jax.ShapeDtypeStruct((n // seq, seq // ATT_BLK, vr, ATT_BLK), BF16)


def _qkv_proj(x, g, w, wvt, gain, cos, sin, bd, *, nq, nk, half, q_scale, v_heads, dv, seq, tm):
    n, d = x.shape
    nt = seq // tm
    row = lambda i: (i, 0)
    const = lambda i: (0, 0)
    vt_spec, vt_shape = _vt_out(n, seq, tm, v_heads * (dv + V_ONES))
    return pl.pallas_call(
        functools.partial(_qkv_kernel, nq=nq, half=half, q_scale=q_scale, v_heads=v_heads, dv=dv),
        grid=(n // tm,),
        in_specs=[pl.BlockSpec((tm, d), row), pl.BlockSpec((1, d), const),
                  pl.BlockSpec(w.shape, const), pl.BlockSpec(wvt.shape, const), pl.BlockSpec(bd.shape, const),
                  pl.BlockSpec(gain.shape, const),
                  pl.BlockSpec((tm, LANES), lambda i: (i % nt, 0)),
                  pl.BlockSpec((tm, LANES), lambda i: (i % nt, 0))],
        out_specs=[pl.BlockSpec((tm, nq), row), pl.BlockSpec((tm, nk), row), vt_spec],
        out_shape=[jax.ShapeDtypeStruct((n, nq), BF16), jax.ShapeDtypeStruct((n, nk), BF16), vt_shape],
        compiler_params=_cparams(("parallel",)),
        name="qkv_proj",
    )(x, g, w, wvt, bd, gain, cos, sin)


def _mla_kernel(x_ref, g_ref, wd_ref, qlg_ref, kvlg_ref, wuq_ref, wuk_ref, wuvt_ref, bd64_ref, bd32_ref,
                gq_ref, gk_ref, cos_ref, sin_ref, q_ref, k_ref, vt_ref, *, scale):
    xn = _rms_rows(x_ref[...], g_ref[...]).astype(BF16)
    c = jnp.dot(xn, wd_ref[...], preferred_element_type=F32)
    cq = _rms_rows(c[:, :B_Q_LORA], qlg_ref[...]).astype(BF16)
    ckv = _rms_rows(c[:, B_Q_LORA:B_Q_LORA + B_KV_LORA], kvlg_ref[...]).astype(BF16)
    q = jnp.dot(cq, wuq_ref[...], preferred_element_type=F32)
    kn = jnp.dot(ckv, wuk_ref[...], preferred_element_type=F32)
    bd64, bd32 = bd64_ref[...], bd32_ref[...]
    cos, sin = cos_ref[...], sin_ref[...]
    gq, gk = gq_ref[...], gk_ref[...]
    hw = B_HEADS * B_NOPE
    qn = _head_rms(q[:, :hw], bd64, gq[:, :hw], B_NOPE) * scale
    qr = _rope_slabs(_head_rms(q[:, hw:], bd32, gq[:, hw:], B_ROPE), cos, sin, B_ROPE // 2) * scale
    kn = _head_rms(kn, bd64, gk[:, :hw], B_NOPE)
    kr_raw = c[:, B_Q_LORA + B_KV_LORA:]
    kr2 = _head_rms(jnp.concatenate([kr_raw, kr_raw], axis=-1), bd32,
                    jnp.concatenate([gk[:, hw:], gk[:, hw:]], axis=-1), B_ROPE)
    kr = _rope_slabs(kr2[:, :LANES], cos, sin, B_ROPE // 2).astype(BF16)
    qn, qr, kn = qn.astype(BF16), qr.astype(BF16), kn.astype(BF16)
    for p in range(B_HEADS // 2):
        sl = slice(p * LANES, (p + 1) * LANES)
        q_ref[:, 2 * p * LANES:(2 * p + 1) * LANES] = qn[:, sl]
        q_ref[:, (2 * p + 1) * LANES:(2 * p + 2) * LANES] = qr[:, sl]
        k_ref[:, 2 * p * LANES:(2 * p + 1) * LANES] = kn[:, sl]
        k_ref[:, (2 * p + 1) * LANES:(2 * p + 2) * LANES] = kr
    vt = lax.dot_general(wuvt_ref[...], ckv, _NT, preferred_element_type=F32)
    _store_vt(vt_ref, vt, heads=B_HEADS, dv=B_V)


def _mla_proj(x, g, wd, qlg, kvlg, wuq, wuk, wuvt, bd64, bd32, gq, gk, cos, sin, *, scale, seq, tm):
    n, d = x.shape
    nt = seq // tm
    row = lambda i: (i, 0)
    const = lambda i: (0, 0)
    full = lambda a: pl.BlockSpec(a.shape, const)
    wq = (B_HEADS // 2) * 2 * LANES
    vt_spec, vt_shape = _vt_out(n, seq, tm, B_HEADS * (B_V + V_ONES))
    return pl.pallas_call(
        functools.partial(_mla_kernel, scale=scale),
        grid=(n // tm,),
        in_specs=[pl.BlockSpec((tm, d), row), full(g), full(wd), full(qlg), full(kvlg), full(wuq), full(wuk),
                  full(wuvt), full(bd64), full(bd32), full(gq), full(gk),
                  pl.BlockSpec((tm, LANES), lambda i: (i % nt, 0)),
                  pl.BlockSpec((tm, LANES), lambda i: (i % nt, 0))],
        out_specs=[pl.BlockSpec((tm, wq), row), pl.BlockSpec((tm, wq), row), vt_spec],
        out_shape=[jax.ShapeDtypeStruct((n, wq), BF16), jax.ShapeDtypeStruct((n, wq), BF16), vt_shape],
        compiler_params=_cparams(("parallel",)),
        name="mla_proj",
    )(x, g, wd, qlg, kvlg, wuq, wuk, wuvt, bd64, bd32, gq, gk, cos, sin)


def _attn_kernel(*refs, mode, qw, dv, nv, tq, nk, band):
    it = iter(refs)
    q_ref, k_ref, vt_ref = next(it), next(it), next(it)
    bias_ref = next(it) if band is not None else None
    if mode == "diff":
        lam_ref, subg_ref = next(it), next(it)
    o_ref = next(it)
    qm_sc, m_sc, acc_sc, sa_sc, sb_sc = next(it), next(it), next(it), next(it), next(it)

    blk = ATT_BLK
    hv = dv + V_ONES
    nsub = tq // blk
    chains = [(r, h) for r in range(nsub) for h in range(2)]
    qi = pl.program_id(2)

    lane = lax.broadcasted_iota(jnp.int32, (1, qw), 1)
    if qw == LANES:
        masks = (lane < HEAD_DIM, lane >= HEAD_DIM)
    else:
        masks = ((lane < HEAD_DIM) | ((lane >= LANES) & (lane < LANES + B_ROPE)),
                 ((lane >= HEAD_DIM) & (lane < LANES)) | ((lane >= LANES + B_ROPE) & (lane < LANES + 2 * B_ROPE)))
    for ci, (r, h) in enumerate(chains):
        qr = q_ref[0, r * blk:(r + 1) * blk, :]
        qm_sc[ci] = jnp.where(masks[h], qr, jnp.zeros_like(qr))
    m_sc[...] = jnp.full(m_sc.shape, NEG_BIG, F32)
    acc_sc[...] = jnp.zeros(acc_sc.shape, F32)

    def score_chain(ci, c, k, s_sc):
        st = lax.dot_general(k, qm_sc[ci], _NT, preferred_element_type=F32)
        if band is not None:
            rel = c - (qi * nsub + chains[ci][0]) + band + 1
            st = st + bias_ref[jnp.clip(rel, 0, 2 * band + 2)]
        s_sc[ci] = st

    def consume_chain(ci, vt, s_sc):
        st = s_sc[ci]
        m_old = m_sc[ci]
        m_new = jnp.maximum(m_old, jnp.max(st, axis=0, keepdims=True))
        p = jnp.exp2(st - m_new).astype(BF16)
        alpha = jnp.exp2(m_old - m_new)
        m_sc[ci] = m_new
        voff = (chains[ci][1] if nv == 2 else 0) * hv
        pv = jnp.dot(vt[voff:voff + hv], p, preferred_element_type=F32)
        acc_sc[ci] = acc_sc[ci] * alpha + pv

    def step(c_cur, s_cur, c_next=None, s_next=None):
        vt = vt_ref[0, c_cur]
        if c_next is not None:
            k = k_ref[0, pl.ds(pl.multiple_of(c_next * blk, blk), blk), :]
        for ci in range(len(chains)):
            if c_next is not None:
                score_chain(ci, c_next, k, s_next)
            consume_chain(ci, vt, s_cur)

    if band is None:
        lo, hi = 0, nk
    else:
        lo, hi = jnp.maximum(qi * nsub - band, 0), jnp.minimum((qi + 1) * nsub + band, nk)
    npairs = (hi - lo - 1) // 2

    def body(i, carry):
        c = lo + 2 * i
        step(c, sa_sc, c + 1, sb_sc)
        step(c + 1, sb_sc, c + 2, sa_sc)
        return carry

    k0 = k_ref[0, pl.ds(pl.multiple_of(lo * blk, blk), blk), :]
    for ci in range(len(chains)):
        score_chain(ci, lo, k0, sa_sc)
    lax.fori_loop(0, npairs, body, 0)
    c_tail = lo + 2 * npairs
    if band is None:
        if (hi - lo) % 2 == 0:
            step(c_tail, sa_sc, c_tail + 1, sb_sc)
            step(c_tail + 1, sb_sc)
        else:
            step(c_tail, sa_sc)
    else:
        two_left = (hi - c_tail) == 2

        @pl.when(two_left)
        def _():
            step(c_tail, sa_sc, c_tail + 1, sb_sc)
            step(c_tail + 1, sb_sc)

        @pl.when(jnp.logical_not(two_left))
        def _():
            step(c_tail, sa_sc)

    for r in range(nsub):
        a, b = acc_sc[2 * r], acc_sc[2 * r + 1]
        oa = a[:dv] / a[dv:dv + 1]
        ob = b[:dv] / b[dv:dv + 1]
        if mode == "pair":
            o = jnp.concatenate([oa, ob], axis=0).T
        else:
            o = (oa - lam_ref[0, 0] * ob).T
            o = _rms_rows(o, subg_ref[...]) * (1.0 - lam_ref[0, 1])
        o_ref[0, r * blk:(r + 1) * blk, :] = o.astype(o_ref.dtype)


def _attention(q, k, vt, *, mode, qw, dv, nv, kmap, vmap, tq, bias=None, band=None, lam=None, subg=None):
    b, s, cq = q.shape
    pairs = cq // qw
    nk = s // ATT_BLK
    hv = dv + V_ONES
    in_specs = [pl.BlockSpec((1, tq, qw), lambda bi, p, i: (bi, i, p)),
                pl.BlockSpec((1, s, qw), lambda bi, p, i: (bi, 0, kmap(p))),
                pl.BlockSpec((1, nk, nv * hv, ATT_BLK), lambda bi, p, i: (bi, 0, vmap(p), 0))]
    args = [q, k, vt]
    if band is not None:
        in_specs.append(pl.BlockSpec(bias.shape, lambda bi, p, i: (0, 0, 0)))
        args.append(bias)
    if mode == "diff":
        in_specs.append(pl.BlockSpec(memory_space=pltpu.SMEM))
        in_specs.append(pl.BlockSpec(subg.shape, lambda bi, p, i: (0, 0)))
        args += [lam, subg]
    nchains = 2 * (tq // ATT_BLK)
    return pl.pallas_call(
        functools.partial(_attn_kernel, mode=mode, qw=qw, dv=dv, nv=nv, tq=tq, nk=nk, band=band),
        grid=(b, pairs, s // tq),
        in_specs=in_specs,
        out_specs=pl.BlockSpec((1, tq, LANES), lambda bi, p, i: (bi, i, p)),
        out_shape=jax.ShapeDtypeStruct((b, s, pairs * LANES), BF16),
        scratch_shapes=[pltpu.VMEM((nchains, ATT_BLK, qw), BF16), pltpu.VMEM((nchains, 1, ATT_BLK), F32),
                        pltpu.VMEM((nchains, hv, ATT_BLK), F32),
                        pltpu.VMEM((nchains, ATT_BLK, ATT_BLK), F32), pltpu.VMEM((nchains, ATT_BLK, ATT_BLK), F32)],
        compiler_params=_cparams(("parallel", "parallel", "arbitrary")),
        name="attention_" + mode,
    )(*args)


def _oproj_kernel(o_ref, w_ref, x_ref, g_ref, wr_ref, xnew_ref, xn_ref, aff_ref):
    xnew = x_ref[...] + jnp.dot(o_ref[...], w_ref[...], preferred_element_type=F32)
    xnew_ref[...] = xnew
    xn = _rms_rows(xnew, g_ref[...])
    xn_ref[...] = xn
    logits = lax.dot_general(wr_ref[...], xn.astype(BF16), _NT, preferred_element_type=F32)
    z = logits - jnp.max(logits, axis=0, keepdims=True)
    e = jnp.exp(z)
    aff_ref[...] = e / jnp.sum(e, axis=0, keepdims=True)


def _oproj(o, w, x, g, wr_t, *, tm):
    n, d = x.shape
    row = lambda i: (i, 0)
    const = lambda i: (0, 0)
    return pl.pallas_call(
        _oproj_kernel,
        grid=(n // tm,),
        in_specs=[pl.BlockSpec((tm, o.shape[1]), row), pl.BlockSpec(w.shape, const), pl.BlockSpec((tm, d), row),
                  pl.BlockSpec((1, d), const), pl.BlockSpec(wr_t.shape, const)],
        out_specs=[pl.BlockSpec((tm, d), row), pl.BlockSpec((tm, d), row),
                   pl.BlockSpec((N_EXPERTS, tm), lambda i: (0, i))],
        out_shape=[jax.ShapeDtypeStruct((n, d), F32), jax.ShapeDtypeStruct((n, d), F32),
                   jax.ShapeDtypeStruct((N_EXPERTS, n), F32)],
        compiler_params=_cparams(("parallel",)),
        name="oproj_router",
    )(o, w, x, g, wr_t)


def _ffn_kernel(xg_ref, wg_ref, wu_ref, wd_ref, gate_ref, y_ref):
    xg = xg_ref[0].astype(BF16)
    a = jnp.dot(xg, wg_ref[0, 0], preferred_element_type=F32)
    u = jnp.dot(xg, wu_ref[0, 0], preferred_element_type=F32)
    h = (a * jax.nn.sigmoid(a) * u).astype(BF16)
    y_ref[0] = jnp.dot(h, wd_ref[0, 0], preferred_element_type=F32) * gate_ref[0]


def _expert_ffn(xg, wg, wu, wd, gate, *, layer, tm):
    e, cap, d = xg.shape
    ff = wg.shape[3]
    return pl.pallas_call(
        _ffn_kernel,
        grid=(e, cap // tm),
        in_specs=[pl.BlockSpec((1, tm, d), lambda ei, i: (ei, i, 0)),
                  pl.BlockSpec((1, 1, d, ff), lambda ei, i: (layer, ei, 0, 0)),
                  pl.BlockSpec((1, 1, d, ff), lambda ei, i: (layer, ei, 0, 0)),
                  pl.BlockSpec((1, 1, ff, d), lambda ei, i: (layer, ei, 0, 0)),
                  pl.BlockSpec((1, tm, 1), lambda ei, i: (ei, i, 0))],
        out_specs=pl.BlockSpec((1, tm, d), lambda ei, i: (ei, i, 0)),
        out_shape=jax.ShapeDtypeStruct((e, cap, d), F32),
        compiler_params=_cparams(("parallel", "arbitrary")),
        name="expert_ffn",
    )(xg, wg, wu, wd, gate)


def _combine_kernel(offs_ref, x_ref, tok_hbm, z_hbm, o_ref, tokbuf, zbuf, sem, *, nsub):
    step = pl.program_id(0)
    blk0 = step * nsub
    c_lo = lax.shift_right_logical(offs_ref[blk0], SEG_ROWS_LOG2)
    c_hi = lax.shift_right_logical(offs_ref[blk0 + nsub] + (SEG_ROWS - 1), SEG_ROWS_LOG2)
    o_ref[...] = x_ref[...]

    def copies(c, slot):
        return (pltpu.make_async_copy(tok_hbm.at[c], tokbuf.at[slot], sem.at[0, slot]),
                pltpu.make_async_copy(z_hbm.at[pl.ds(pl.multiple_of(c * SEG_ROWS, SEG_ROWS), SEG_ROWS)],
                                      zbuf.at[slot], sem.at[1, slot]))

    for j in range(SEG_BUFS - 1):
        @pl.when(c_lo + j < c_hi)
        def _():
            for cp in copies(c_lo + j, j):
                cp.start()

    def body(c, carry):
        slot = lax.rem(c - c_lo, SEG_BUFS)
        for cp in copies(c, slot):
            cp.wait()
        ahead = c + (SEG_BUFS - 1)

        @pl.when(ahead < c_hi)
        def _():
            for cp in copies(ahead, lax.rem(ahead - c_lo, SEG_BUFS)):
                cp.start()

        tok = tokbuf[slot]
        z = zbuf[slot]
        zh = z.astype(BF16)
        zl = (z - zh.astype(F32)).astype(BF16)
        row0 = c * SEG_ROWS
        for u in range(nsub):
            lo_u, hi_u = offs_ref[blk0 + u], offs_ref[blk0 + u + 1]

            @pl.when((lo_u < row0 + SEG_ROWS) & (hi_u > row0))
            def _():
                tid = (blk0 + u) * SEG_TOK + lax.broadcasted_iota(jnp.int32, (SEG_TOK, SEG_ROWS), 0)
                onehot = jnp.where(tok == tid, 1.0, 0.0).astype(BF16)
                upd = (jnp.dot(onehot, zh, preferred_element_type=F32)
                       + jnp.dot(onehot, zl, preferred_element_type=F32))
                o_ref[u * SEG_TOK:(u + 1) * SEG_TOK, :] += upd
        return carry

    lax.fori_loop(c_lo, c_hi, body, 0)


def _combine(offs, x, tok, z):
    n, d = x.shape
    tn = _pick(n, SEG_STEP_TOK)
    return pl.pallas_call(
        functools.partial(_combine_kernel, nsub=tn // SEG_TOK),
        grid_spec=pltpu.PrefetchScalarGridSpec(
            num_scalar_prefetch=1,
            grid=(n // tn,),
            in_specs=[pl.BlockSpec((tn, d), lambda i, offs: (i, 0)),
                      pl.BlockSpec(memory_space=pl.ANY), pl.BlockSpec(memory_space=pl.ANY)],
            out_specs=pl.BlockSpec((tn, d), lambda i, offs: (i, 0)),
            scratch_shapes=[pltpu.VMEM((SEG_BUFS, 1, SEG_ROWS), jnp.int32),
                            pltpu.VMEM((SEG_BUFS, SEG_ROWS, d), F32),
                            pltpu.SemaphoreType.DMA((2, SEG_BUFS))]),
        out_shape=jax.ShapeDtypeStruct((n, d), F32),
        compiler_params=_cparams(("arbitrary",)),
        name="combine_segsum",
    )(offs, x, tok, z)


def _block_diag(hd):
    i = jnp.arange(MXU_DIM)
    return (i[:, None] // hd == i[None, :] // hd).astype(BF16)


def _rope_tables_std(seq, dim):
    half = dim // 2
    inv_freq = ROPE_THETA ** (-jnp.arange(half, dtype=F32) * 2.0 / dim)
    ang = jnp.arange(seq).astype(F32)[:, None] * inv_freq[None, :]
    cos, sin = jnp.cos(ang), jnp.sin(ang)
    reps = LANES // dim
    return (jnp.tile(jnp.concatenate([cos, cos], axis=-1), (1, reps)),
            jnp.tile(jnp.concatenate([-sin, sin], axis=-1), (1, reps)))


def _rope_tables_axial(seq):
    sub = HEAD_DIM // 2
    half = sub // 2
    inv_freq = ROPE_THETA ** (-jnp.arange(half, dtype=F32) * 2.0 / sub)
    n_rows = seq // GRID_W
    rows = jnp.repeat(jnp.arange(n_rows), GRID_W).astype(F32)
    cols = jnp.tile(jnp.arange(GRID_W), n_rows).astype(F32)
    ar, ac = rows[:, None] * inv_freq[None, :], cols[:, None] * inv_freq[None, :]
    cos = jnp.concatenate([jnp.cos(ar), jnp.cos(ar), jnp.cos(ac), jnp.cos(ac)], axis=-1)
    sin = jnp.concatenate([-jnp.sin(ar), jnp.sin(ar), -jnp.sin(ac), jnp.sin(ac)], axis=-1)
    return jnp.tile(cos, (1, 2)), jnp.tile(sin, (1, 2))


def _dilated_bias(band):
    t = ATT_BLK
    rel = jnp.arange(-band - 1, band + 2)[:, None, None] * t
    d = rel + jnp.arange(t)[None, :, None] - jnp.arange(t)[None, None, :]
    cnt = jnp.zeros(d.shape, F32)
    for window, dil in DILATED_BRANCHES:
        cnt = cnt + ((d % dil == 0) & (jnp.abs(d) <= window // 2)).astype(F32)
    return jnp.where(cnt > 0, jnp.log2(jnp.maximum(cnt, 1.0)), NEG_BIG)


def _tile_gain(g, reps):
    return jnp.tile(g.astype(F32), reps)[None, :]


def _pick(n, pref):
    t = min(n, pref)
    while n % t:
        t //= 2
    return t


def _mixer_a(x, b, s, g, w_qkv, q_g, k_g):
    nq = A_HEADS * HEAD_DIM
    nkv = A_KV_HEADS * HEAD_DIM
    wq, wk, wv = w_qkv[:, :nq], w_qkv[:, nq:nq + nkv], w_qkv[:, nq + nkv:]
    wk2 = jnp.repeat(wk.reshape(D_MODEL, A_KV_HEADS, 1, HEAD_DIM), 2, axis=2).reshape(D_MODEL, -1)
    w = jnp.concatenate([wq, wk2], axis=1).astype(BF16)
    nk = 2 * nkv
    gain = jnp.concatenate([_tile_gain(q_g, nq // HEAD_DIM), _tile_gain(k_g, nk // HEAD_DIM)], axis=1)
    cos, sin = _rope_tables_axial(s)
    q, k, vt = _qkv_proj(x, g, w, wv.T.astype(BF16), gain, cos, sin, _block_diag(HEAD_DIM), nq=nq, nk=nk,
                         half=HEAD_DIM // 4, q_scale=HEAD_DIM ** -0.5 * LOG2E, v_heads=A_KV_HEADS, dv=HEAD_DIM,
                         seq=s, tm=_pick(s, 512))
    group_pairs = (A_HEADS // A_KV_HEADS) // 2
    return _attention(q.reshape(b, s, -1), k.reshape(b, s, -1), vt, mode="pair", qw=LANES, dv=HEAD_DIM, nv=1,
                      kmap=lambda p: p // group_pairs, vmap=lambda p: p // group_pairs, tq=_pick(s, ATT_TQ))


def _mixer_b(x, b, s, g, w_down, q_lat_g, kv_lat_g, w_uq, w_ukv, q_nope_g, q_rope_g, k_nope_g, k_rope_g):
    zpad = lambda a, n: jnp.concatenate([a, jnp.zeros(a.shape[:-1] + (n,), a.dtype)], axis=-1)
    w_kr = w_down[:, B_Q_LORA + B_KV_LORA:]
    wd = jnp.concatenate([w_down[:, :B_Q_LORA + B_KV_LORA], zpad(jnp.concatenate([w_kr, w_kr], axis=1), 2 * B_ROPE)],
                         axis=1).astype(BF16)
    uq = w_uq.reshape(B_Q_LORA, B_HEADS, B_NOPE + B_ROPE)
    uq_n = uq[:, :, :B_NOPE].reshape(B_Q_LORA, -1)
    uq_r = zpad(uq[:, :, B_NOPE:].reshape(B_Q_LORA, B_HEADS // 2, 2 * B_ROPE), LANES - 2 * B_ROPE).reshape(B_Q_LORA, -1)
    wuq = jnp.concatenate([uq_n, uq_r], axis=1).astype(BF16)
    ukv = w_ukv.reshape(B_KV_LORA, B_HEADS, B_NOPE + B_V)
    wuk = ukv[:, :, :B_NOPE].reshape(B_KV_LORA, -1).astype(BF16)
    wuvt = ukv[:, :, B_NOPE:].reshape(B_KV_LORA, -1).T.astype(BF16)
    gq = jnp.concatenate([_tile_gain(q_nope_g, B_HEADS), _tile_gain(q_rope_g, B_HEADS * B_NOPE // B_ROPE)], axis=1)
    gk = jnp.concatenate([_tile_gain(k_nope_g, B_HEADS), _tile_gain(k_rope_g, LANES // B_ROPE)], axis=1)
    cos, sin = _rope_tables_std(s, B_ROPE)
    q, k, vt = _mla_proj(x, g, wd, q_lat_g.astype(F32)[None, :], kv_lat_g.astype(F32)[None, :], wuq, wuk, wuvt,
                         _block_diag(B_NOPE), _block_diag(B_ROPE), gq, gk, cos, sin,
                         scale=(B_NOPE + B_ROPE) ** -0.5 * LOG2E, seq=s, tm=_pick(s, 512))
    return _attention(q.reshape(b, s, -1), k.reshape(b, s, -1), vt, mode="pair", qw=2 * LANES, dv=B_V, nv=2,
                      kmap=lambda p: p, vmap=lambda p: p, tq=_pick(s, ATT_TQ))


def _mixer_c(x, b, s, g, w_qkv, q_g, k_g, lq1, lk1, lq2, lk2, subln_g, layer_idx):
    nq = 2 * C_HEADS * HEAD_DIM
    gain = jnp.concatenate([_tile_gain(q_g, nq // HEAD_DIM), _tile_gain(k_g, nq // HEAD_DIM)], axis=1)
    cos, sin = _rope_tables_std(s, HEAD_DIM)
    q, k, vt = _qkv_proj(x, g, w_qkv[:, :2 * nq].astype(BF16), w_qkv[:, 2 * nq:].T.astype(BF16), gain, cos, sin,
                         _block_diag(HEAD_DIM), nq=nq, nk=nq, half=HEAD_DIM // 2, q_scale=HEAD_DIM ** -0.5 * LOG2E,
                         v_heads=C_HEADS, dv=2 * HEAD_DIM, seq=s, tm=_pick(s, 512))
    lam_init = LAMBDA_INIT_BASE - LAMBDA_INIT_AMP * math.exp(-LAMBDA_INIT_RATE * layer_idx)
    lam = (jnp.exp(jnp.sum(lq1.astype(F32) * lk1.astype(F32))) - jnp.exp(jnp.sum(lq2.astype(F32) * lk2.astype(F32)))
           + lam_init)
    lam_arr = jnp.stack([lam, jnp.asarray(lam_init, F32)]).reshape(1, 2).astype(F32)
    return _attention(q.reshape(b, s, -1), k.reshape(b, s, -1), vt, mode="diff", qw=LANES, dv=2 * HEAD_DIM, nv=1,
                      kmap=lambda p: p, vmap=lambda p: p, tq=_pick(s, ATT_TQ), lam=lam_arr,
                      subg=subln_g.astype(F32)[None, :])


def _mixer_d(x, b, s, g, w_qkv, q_g, k_g):
    nq = D_HEADS * HEAD_DIM
    gain = jnp.concatenate([_tile_gain(q_g, D_HEADS), _tile_gain(k_g, D_HEADS)], axis=1)
    cos, sin = _rope_tables_std(s, HEAD_DIM)
    q, k, vt = _qkv_proj(x, g, w_qkv[:, :2 * nq].astype(BF16), w_qkv[:, 2 * nq:].T.astype(BF16), gain, cos, sin,
                         _block_diag(HEAD_DIM), nq=nq, nk=nq, half=HEAD_DIM // 2, q_scale=HEAD_DIM ** -0.5 * LOG2E,
                         v_heads=D_HEADS, dv=HEAD_DIM, seq=s, tm=_pick(s, 512))
    reach = max(w // 2 for w, _ in DILATED_BRANCHES)
    band = -(-reach // ATT_BLK)
    return _attention(q.reshape(b, s, -1), k.reshape(b, s, -1), vt, mode="pair", qw=LANES, dv=HEAD_DIM, nv=2,
                      kmap=lambda p: p, vmap=lambda p: p, tq=_pick(s, 2 * ATT_BLK), bias=_dilated_bias(band), band=band)


def _ec_ffn(xnew, xn, aff, wg, wu, wd, layer):
    n, d = xnew.shape
    cap = (EC_CAPACITY * n) // N_EXPERTS
    gate, idx = lax.top_k(aff, cap)
    y = _expert_ffn(xn[idx], wg, wu, wd, gate[..., None], layer=layer, tm=_pick(cap, 512))
    flat = idx.reshape(-1)
    tok_sorted, order = lax.sort_key_val(flat, jnp.arange(flat.size, dtype=jnp.int32))
    bounds = jnp.arange(0, n + 1, SEG_TOK, dtype=jnp.int32)
    offs = jnp.sum((tok_sorted[None, :] < bounds[:, None]).astype(jnp.int32), axis=1)
    return _combine(offs, xnew, tok_sorted.reshape(-1, 1, SEG_ROWS), y.reshape(-1, d)[order])


def kernel(x_prompt, x_sample, norm_mix_g, norm_ffn_g, a_w_qkv, a_q_norm_g, a_k_norm_g, a_w_o, b_w_down, b_q_lat_norm_g, b_kv_lat_norm_g, b_w_uq, b_w_ukv, b_q_nope_norm_g, b_q_rope_norm_g, b_k_nope_norm_g, b_k_rope_norm_g, b_w_o, c_w_qkv, c_q_norm_g, c_k_norm_g, c_lambda_q1, c_lambda_k1, c_lambda_q2, c_lambda_k2, c_subln_g, c_w_o, d_w_qkv, d_q_norm_g, d_k_norm_g, d_w_o, ec_w_router, ec_w_gate, ec_w_up, ec_w_down):
    depth = norm_mix_g.shape[0]
    wg_all, wu_all, wd_all = ec_w_gate.astype(BF16), ec_w_up.astype(BF16), ec_w_down.astype(BF16)

    def trunk(x3):
        b, s, d = x3.shape
        x = x3.reshape(b * s, d)
        for i in range(depth):
            m, j = i % N_MIXERS, i // N_MIXERS
            g = norm_mix_g[i].astype(F32)[None, :]
            if m == 0:
                o, w_o = _mixer_a(x, b, s, g, a_w_qkv[j], a_q_norm_g[j], a_k_norm_g[j]), a_w_o[j]
            elif m == 1:
                o = _mixer_b(x, b, s, g, b_w_down[j], b_q_lat_norm_g[j], b_kv_lat_norm_g[j], b_w_uq[j], b_w_ukv[j],
                             b_q_nope_norm_g[j], b_q_rope_norm_g[j], b_k_nope_norm_g[j], b_k_rope_norm_g[j])
                w_o = b_w_o[j]
            elif m == 2:
                o = _mixer_c(x, b, s, g, c_w_qkv[j], c_q_norm_g[j], c_k_norm_g[j], c_lambda_q1[j], c_lambda_k1[j],
                             c_lambda_q2[j], c_lambda_k2[j], c_subln_g[j], i)
                w_o = c_w_o[j]
            else:
                o, w_o = _mixer_d(x, b, s, g, d_w_qkv[j], d_q_norm_g[j], d_k_norm_g[j]), d_w_o[j]
            xnew, xn, aff = _oproj(o.reshape(b * s, -1), w_o.astype(BF16), x, norm_ffn_g[i].astype(F32)[None, :],
                                   ec_w_router[i].T.astype(BF16), tm=_pick(b * s, 512))
            x = _ec_ffn(xnew, xn, aff, wg_all, wu_all, wd_all, i)
        return x.reshape(b, s, d)

    return (trunk(x_prompt), trunk(x_sample))
```

```python
import functools
import math

import jax
import jax.numpy as jnp
from jax import lax
from jax.experimental import pallas as pl
from jax.experimental.pallas import tpu as pltpu

F32 = jnp.float32
BF16 = jnp.bfloat16

D_MODEL = 1024
HEAD_DIM = 64
GRID_W = 64
ROPE_THETA = 10000.0
RMS_EPS = 1e-6
N_MIXERS = 4
A_HEADS, A_KV_HEADS = 16, 4
B_HEADS, B_Q_LORA, B_KV_LORA, B_NOPE, B_ROPE, B_V = 16, 384, 256, 64, 32, 64
C_HEADS = 8
LAMBDA_INIT_BASE, LAMBDA_INIT_AMP, LAMBDA_INIT_RATE = 0.8, 0.6, 0.3
D_HEADS = 16
DILATED_BRANCHES = ((128, 1), (512, 4), (2048, 16))
N_EXPERTS = 16
EC_CAPACITY = 2
EXPERT_FF = 2 * D_MODEL

LANES = 128
MXU_DIM = 256
VMEM_LIMIT = 56 * 1024 * 1024
V_ONES = 16
ATT_BLK = MXU_DIM
ATT_TQ = 4 * ATT_BLK
SEG_TOK = MXU_DIM
SEG_ROWS_LOG2 = 8
SEG_ROWS = 1 << SEG_ROWS_LOG2
SEG_STEP_TOK = 8 * SEG_TOK
SEG_BUFS = 4
NEG_BIG = -1e30
LOG2E = math.log2(math.e)


def _cparams(sem):
    return pltpu.CompilerParams(dimension_semantics=sem, vmem_limit_bytes=VMEM_LIMIT)


_NT = (((1,), (1,)), ((), ()))


def _rms_rows(x, g):
    return x * lax.rsqrt(jnp.mean(x * x, axis=-1, keepdims=True) + RMS_EPS) * g


def _head_rms(t, bd, gain, hd):
    outs = []
    for j in range(t.shape[1] // MXU_DIM):
        tj = t[:, j * MXU_DIM:(j + 1) * MXU_DIM]
        ss = jnp.dot((tj * tj).astype(BF16), bd, preferred_element_type=F32)
        outs.append(tj * lax.rsqrt(ss * (1.0 / hd) + RMS_EPS))
    y = outs[0] if len(outs) == 1 else jnp.concatenate(outs, axis=-1)
    return y * gain


def _rope_slabs(y, cos, sin, half):
    lane = lax.broadcasted_iota(jnp.int32, (1, LANES), 1)
    first = (lane % (2 * half)) < half
    outs = []
    for j in range(y.shape[1] // LANES):
        yj = y[:, j * LANES:(j + 1) * LANES]
        swapped = jnp.where(first, pltpu.roll(yj, LANES - half, 1), pltpu.roll(yj, half, 1))
        outs.append(yj * cos + swapped * sin)
    return outs[0] if len(outs) == 1 else jnp.concatenate(outs, axis=-1)


def _store_vt(vt_ref, vt, *, heads, dv):
    hv = dv + V_ONES
    ones = jnp.ones((V_ONES, ATT_BLK), BF16)
    vt = vt.astype(BF16)
    for j in range(vt.shape[1] // ATT_BLK):
        for h in range(heads):
            vt_ref[0, j, h * hv:h * hv + dv, :] = vt[h * dv:(h + 1) * dv, j * ATT_BLK:(j + 1) * ATT_BLK]
            vt_ref[0, j, h * hv + dv:(h + 1) * hv, :] = ones


def _qkv_kernel(x_ref, g_ref, w_ref, wvt_ref, bd_ref, gain_ref, cos_ref, sin_ref, q_ref, k_ref, vt_ref,
                *, nq, half, q_scale, v_heads, dv):
    xn = _rms_rows(x_ref[...], g_ref[...]).astype(BF16)
    y = jnp.dot(xn, w_ref[...], preferred_element_type=F32)
    bd = bd_ref[...]
    cos, sin = cos_ref[...], sin_ref[...]
    gain = gain_ref[...]
    q = _rope_slabs(_head_rms(y[:, :nq], bd, gain[:, :nq], HEAD_DIM), cos, sin, half) * q_scale
    k = _rope_slabs(_head_rms(y[:, nq:], bd, gain[:, nq:], HEAD_DIM), cos, sin, half)
    q_ref[...] = q.astype(BF16)
    k_ref[...] = k.astype(BF16)
    vt = lax.dot_general(wvt_ref[...], xn, _NT, preferred_element_type=F32)
    _store_vt(vt_ref, vt, heads=v_heads, dv=dv)


def _vt_out(n, seq, tm, vr):
    nt = seq // tm
    spec = pl.BlockSpec((1, tm // ATT_BLK, vr, ATT_BLK), lambda i: (i // nt, i % nt, 0, 0))
    return spec, jax.ShapeDtypeStruct((n // seq, seq // ATT_BLK, vr, ATT_BLK), BF16)


def _qkv_proj(x, g, w, wvt, gain, cos, sin, bd, *, nq, nk, half, q_scale, v_heads, dv, seq, tm):
    n, d = x.shape
    nt = seq // tm
    row = lambda i: (i, 0)
    const = lambda i: (0, 0)
    vt_spec, vt_shape = _vt_out(n, seq, tm, v_heads * (dv + V_ONES))
    return pl.pallas_call(
        functools.partial(_qkv_kernel, nq=nq, half=half, q_scale=q_scale, v_heads=v_heads, dv=dv),
        grid=(n // tm,),
        in_specs=[pl.BlockSpec((tm, d), row), pl.BlockSpec((1, d), const),
                  pl.BlockSpec(w.shape, const), pl.BlockSpec(wvt.shape, const), pl.BlockSpec(bd.shape, const),
                  pl.BlockSpec(gain.shape, const),
                  pl.BlockSpec((tm, LANES), lambda i: (i % nt, 0)),
                  pl.BlockSpec((tm, LANES), lambda i: (i % nt, 0))],
        out_specs=[pl.BlockSpec((tm, nq), row), pl.BlockSpec((tm, nk), row), vt_spec],
        out_shape=[jax.ShapeDtypeStruct((n, nq), BF16), jax.ShapeDtypeStruct((n, nk), BF16), vt_shape],
        compiler_params=_cparams(("parallel",)),
        name="qkv_proj",
    )(x, g, w, wvt, bd, gain, cos, sin)


def _mla_kernel(x_ref, g_ref, wd_ref, qlg_ref, kvlg_ref, wuq_ref, wuk_ref, wuvt_ref, bd64_ref, bd32_ref,
                gq_ref, gk_ref, cos_ref, sin_ref, q_ref, k_ref, vt_ref, *, scale):
    xn = _rms_rows(x_ref[...], g_ref[...]).astype(BF16)
    c = jnp.dot(xn, wd_ref[...], preferred_element_type=F32)
    cq = _rms_rows(c[:, :B_Q_LORA], qlg_ref[...]).astype(BF16)
    ckv = _rms_rows(c[:, B_Q_LORA:B_Q_LORA + B_KV_LORA], kvlg_ref[...]).astype(BF16)
    q = jnp.dot(cq, wuq_ref[...], preferred_element_type=F32)
    kn = jnp.dot(ckv, wuk_ref[...], preferred_element_type=F32)
    bd64, bd32 = bd64_ref[...], bd32_ref[...]
    cos, sin = cos_ref[...], sin_ref[...]
    gq, gk = gq_ref[...], gk_ref[...]
    hw = B_HEADS * B_NOPE
    qn = _head_rms(q[:, :hw], bd64, gq[:, :hw], B_NOPE) * scale
    qr = _rope_slabs(_head_rms(q[:, hw:], bd32, gq[:, hw:], B_ROPE), cos, sin, B_ROPE // 2) * scale
    kn = _head_rms(kn, bd64, gk[:, :hw], B_NOPE)
    kr_raw = c[:, B_Q_LORA + B_KV_LORA:]
    kr2 = _head_rms(jnp.concatenate([kr_raw, kr_raw], axis=-1), bd32,
                    jnp.concatenate([gk[:, hw:], gk[:, hw:]], axis=-1), B_ROPE)
    kr = _rope_slabs(kr2[:, :LANES], cos, sin, B_ROPE // 2).astype(BF16)
    qn, qr, kn = qn.astype(BF16), qr.astype(BF16), kn.astype(BF16)
    for p in range(B_HEADS // 2):
        sl = slice(p * LANES, (p + 1) * LANES)
        q_ref[:, 2 * p * LANES:(2 * p + 1) * LANES] = qn[:, sl]
        q_ref[:, (2 * p + 1) * LANES:(2 * p + 2) * LANES] = qr[:, sl]
        k_ref[:, 2 * p * LANES:(2 * p + 1) * LANES] = kn[:, sl]
        k_ref[:, (2 * p + 1) * LANES:(2 * p + 2) * LANES] = kr
    vt = lax.dot_general(wuvt_ref[...], ckv, _NT, preferred_element_type=F32)
    _store_vt(vt_ref, vt, heads=B_HEADS, dv=B_V)


def _mla_proj(x, g, wd, qlg, kvlg, wuq, wuk, wuvt, bd64, bd32, gq, gk, cos, sin, *, scale, seq, tm):
    n, d = x.shape
    nt = seq // tm
    row = lambda i: (i, 0)
    const = lambda i: (0, 0)
    full = lambda a: pl.BlockSpec(a.shape, const)
    wq = (B_HEADS // 2) * 2 * LANES
    vt_spec, vt_shape = _vt_out(n, seq, tm, B_HEADS * (B_V + V_ONES))
    return pl.pallas_call(
        functools.partial(_mla_kernel, scale=scale),
        grid=(n // tm,),
        in_specs=[pl.BlockSpec((tm, d), row), full(g), full(wd), full(qlg), full(kvlg), full(wuq), full(wuk),
                  full(wuvt), full(bd64), full(bd32), full(gq), full(gk),
                  pl.BlockSpec((tm, LANES), lambda i: (i % nt, 0)),
                  pl.BlockSpec((tm, LANES), lambda i: (i % nt, 0))],
        out_specs=[pl.BlockSpec((tm, wq), row), pl.BlockSpec((tm, wq), row), vt_spec],
        out_shape=[jax.ShapeDtypeStruct((n, wq), BF16), jax.ShapeDtypeStruct((n, wq), BF16), vt_shape],
        compiler_params=_cparams(("parallel",)),
        name="mla_proj",
    )(x, g, wd, qlg, kvlg, wuq, wuk, wuvt, bd64, bd32, gq, gk, cos, sin)


def _attn_kernel(*refs, mode, qw, dv, nv, tq, nk, band):
    it = iter(refs)
    q_ref, k_ref, vt_ref = next(it), next(it), next(it)
    bias_ref = next(it) if band is not None else None
    if mode == "diff":
        lam_ref, subg_ref = next(it), next(it)
    o_ref = next(it)
    qm_sc, m_sc, acc_sc, sa_sc, sb_sc = next(it), next(it), next(it), next(it), next(it)

    blk = ATT_BLK
    hv = dv + V_ONES
    nsub = tq // blk
    chains = [(r, h) for r in range(nsub) for h in range(2)]
    qi = pl.program_id(2)

    lane = lax.broadcasted_iota(jnp.int32, (1, qw), 1)
    if qw == LANES:
        masks = (lane < HEAD_DIM, lane >= HEAD_DIM)
    else:
        masks = ((lane < HEAD_DIM) | ((lane >= LANES) & (lane < LANES + B_ROPE)),
                 ((lane >= HEAD_DIM) & (lane < LANES)) | ((lane >= LANES + B_ROPE) & (lane < LANES + 2 * B_ROPE)))
    for ci, (r, h) in enumerate(chains):
        qr = q_ref[0, r * blk:(r + 1) * blk, :]
        qm_sc[ci] = jnp.where(masks[h], qr, jnp.zeros_like(qr))
    m_sc[...] = jnp.full(m_sc.shape, NEG_BIG, F32)
    acc_sc[...] = jnp.zeros(acc_sc.shape, F32)

    def score_chain(ci, c, k, s_sc):
        st = lax.dot_general(k, qm_sc[ci], _NT, preferred_element_type=F32)
        if band is not None:
            rel = c - (qi * nsub + chains[ci][0]) + band + 1
            st = st + bias_ref[jnp.clip(rel, 0, 2 * band + 2)]
        s_sc[ci] = st

    def consume_chain(ci, vt, s_sc):
        st = s_sc[ci]
        m_old = m_sc[ci]
        m_new = jnp.maximum(m_old, jnp.max(st, axis=0, keepdims=True))
        p = jnp.exp2(st - m_new).astype(BF16)
        alpha = jnp.exp2(m_old - m_new)
        m_sc[ci] = m_new
        voff = (chains[ci][1] if nv == 2 else 0) * hv
        pv = jnp.dot(vt[voff:voff + hv], p, preferred_element_type=F32)
        acc_sc[ci] = acc_sc[ci] * alpha + pv

    def step(c_cur, s_cur, c_next=None, s_next=None):
        vt = vt_ref[0, c_cur]
        if c_next is not None:
            k = k_ref[0, pl.ds(pl.multiple_of(c_next * blk, blk), blk), :]
        for ci in range(len(chains)):
            if c_next is not None:
                score_chain(ci, c_next, k, s_next)
            consume_chain(ci, vt, s_cur)

    if band is None:
        lo, hi = 0, nk
    else:
        lo, hi = jnp.maximum(qi * nsub - band, 0), jnp.minimum((qi + 1) * nsub + band, nk)
    npairs = (hi - lo - 1) // 2

    def body(i, carry):
        c = lo + 2 * i
        step(c, sa_sc, c + 1, sb_sc)
        step(c + 1, sb_sc, c + 2, sa_sc)
        return carry

    k0 = k_ref[0, pl.ds(pl.multiple_of(lo * blk, blk), blk), :]
    for ci in range(len(chains)):
        score_chain(ci, lo, k0, sa_sc)
    lax.fori_loop(0, npairs, body, 0)
    c_tail = lo + 2 * npairs
    if band is None:
        if (hi - lo) % 2 == 0:
            step(c_tail, sa_sc, c_tail + 1, sb_sc)
            step(c_tail + 1, sb_sc)
        else:
            step(c_tail, sa_sc)
    else:
        two_left = (hi - c_tail) == 2

        @pl.when(two_left)
        def _():
            step(c_tail, sa_sc, c_tail + 1, sb_sc)
            step(c_tail + 1, sb_sc)

        @pl.when(jnp.logical_not(two_left))
        def _():
            step(c_tail, sa_sc)

    for r in range(nsub):
        a, b = acc_sc[2 * r], acc_sc[2 * r + 1]
        oa = a[:dv] / a[dv:dv + 1]
        ob = b[:dv] / b[dv:dv + 1]
        if mode == "pair":
            o = jnp.concatenate([oa, ob], axis=0).T
        else:
            o = (oa - lam_ref[0, 0] * ob).T
            o = _rms_rows(o, subg_ref[...]) * (1.0 - lam_ref[0, 1])
        o_ref[0, r * blk:(r + 1) * blk, :] = o.astype(o_ref.dtype)


def _attention(q, k, vt, *, mode, qw, dv, nv, kmap, vmap, tq, bias=None, band=None, lam=None, subg=None):
    b, s, cq = q.shape
    pairs = cq // qw
    nk = s // ATT_BLK
    hv = dv + V_ONES
    in_specs = [pl.BlockSpec((1, tq, qw), lambda bi, p, i: (bi, i, p)),
                pl.BlockSpec((1, s, qw), lambda bi, p, i: (bi, 0, kmap(p))),
                pl.BlockSpec((1, nk, nv * hv, ATT_BLK), lambda bi, p, i: (bi, 0, vmap(p), 0))]
    args = [q, k, vt]
    if band is not None:
        in_specs.append(pl.BlockSpec(bias.shape, lambda bi, p, i: (0, 0, 0)))
        args.append(bias)
    if mode == "diff":
        in_specs.append(pl.BlockSpec(memory_space=pltpu.SMEM))
        in_specs.append(pl.BlockSpec(subg.shape, lambda bi, p, i: (0, 0)))
        args += [lam, subg]
    nchains = 2 * (tq // ATT_BLK)
    return pl.pallas_call(
        functools.partial(_attn_kernel, mode=mode, qw=qw, dv=dv, nv=nv, tq=tq, nk=nk, band=band),
        grid=(b, pairs, s // tq),
        in_specs=in_specs,
        out_specs=pl.BlockSpec((1, tq, LANES), lambda bi, p, i: (bi, i, p)),
        out_shape=jax.ShapeDtypeStruct((b, s, pairs * LANES), BF16),
        scratch_shapes=[pltpu.VMEM((nchains, ATT_BLK, qw), BF16), pltpu.VMEM((nchains, 1, ATT_BLK), F32),
                        pltpu.VMEM((nchains, hv, ATT_BLK), F32),
                        pltpu.VMEM((nchains, ATT_BLK, ATT_BLK), F32), pltpu.VMEM((nchains, ATT_BLK, ATT_BLK), F32)],
        compiler_params=_cparams(("parallel", "parallel", "arbitrary")),
        name="attention_" + mode,
    )(*args)


def _oproj_kernel(o_ref, w_ref, x_ref, g_ref, wr_ref, xnew_ref, xn_ref, aff_ref):
    xnew = x_ref[...] + jnp.dot(o_ref[...], w_ref[...], preferred_element_type=F32)
    xnew_ref[...] = xnew
    xn = _rms_rows(xnew, g_ref[...])
    xn_ref[...] = xn
    logits = lax.dot_general(wr_ref[...], xn.astype(BF16), _NT, preferred_element_type=F32)
    z = logits - jnp.max(logits, axis=0, keepdims=True)
    e = jnp.exp(z)
    aff_ref[...] = e / jnp.sum(e, axis=0, keepdims=True)


def _oproj(o, w, x, g, wr_t, *, tm):
    n, d = x.shape
    row = lambda i: (i, 0)
    const = lambda i: (0, 0)
    return pl.pallas_call(
        _oproj_kernel,
        grid=(n // tm,),
        in_specs=[pl.BlockSpec((tm, o.shape[1]), row), pl.BlockSpec(w.shape, const), pl.BlockSpec((tm, d), row),
                  pl.BlockSpec((1, d), const), pl.BlockSpec(wr_t.shape, const)],
        out_specs=[pl.BlockSpec((tm, d), row), pl.BlockSpec((tm, d), row),
                   pl.BlockSpec((N_EXPERTS, tm), lambda i: (0, i))],
        out_shape=[jax.ShapeDtypeStruct((n, d), F32), jax.ShapeDtypeStruct((n, d), F32),
                   jax.ShapeDtypeStruct((N_EXPERTS, n), F32)],
        compiler_params=_cparams(("parallel",)),
        name="oproj_router",
    )(o, w, x, g, wr_t)


def _ffn_kernel(xg_ref, wg_ref, wu_ref, wd_ref, gate_ref, y_ref):
    xg = xg_ref[0].astype(BF16)
    a = jnp.dot(xg, wg_ref[0, 0], preferred_element_type=F32)
    u = jnp.dot(xg, wu_ref[0, 0], preferred_element_type=F32)
    h = (a * jax.nn.sigmoid(a) * u).astype(BF16)
    y_ref[0] = jnp.dot(h, wd_ref[0, 0], preferred_element_type=F32) * gate_ref[0]


def _expert_ffn(xg, wg, wu, wd, gate, *, layer, tm):
    e, cap, d = xg.shape
    ff = wg.shape[3]
    return pl.pallas_call(
        _ffn_kernel,
        grid=(e, cap // tm),
        in_specs=[pl.BlockSpec((1, tm, d), lambda ei, i: (ei, i, 0)),
                  pl.BlockSpec((1, 1, d, ff), lambda ei, i: (layer, ei, 0, 0)),
                  pl.BlockSpec((1, 1, d, ff), lambda ei, i: (layer, ei, 0, 0)),
                  pl.BlockSpec((1, 1, ff, d), lambda ei, i: (layer, ei, 0, 0)),
                  pl.BlockSpec((1, tm, 1), lambda ei, i: (ei, i, 0))],
        out_specs=pl.BlockSpec((1, tm, d), lambda ei, i: (ei, i, 0)),
        out_shape=jax.ShapeDtypeStruct((e, cap, d), F32),
        compiler_params=_cparams(("parallel", "arbitrary")),
        name="expert_ffn",
    )(xg, wg, wu, wd, gate)


def _combine_kernel(offs_ref, x_ref, tok_hbm, z_hbm, o_ref, tokbuf, zbuf, sem, *, nsub):
    step = pl.program_id(0)
    blk0 = step * nsub
    c_lo = lax.shift_right_logical(offs_ref[blk0], SEG_ROWS_LOG2)
    c_hi = lax.shift_right_logical(offs_ref[blk0 + nsub] + (SEG_ROWS - 1), SEG_ROWS_LOG2)
    o_ref[...] = x_ref[...]

    def copies(c, slot):
        return (pltpu.make_async_copy(tok_hbm.at[c], tokbuf.at[slot], sem.at[0, slot]),
                pltpu.make_async_copy(z_hbm.at[pl.ds(pl.multiple_of(c * SEG_ROWS, SEG_ROWS), SEG_ROWS)],
                                      zbuf.at[slot], sem.at[1, slot]))

    for j in range(SEG_BUFS - 1):
        @pl.when(c_lo + j < c_hi)
        def _():
            for cp in copies(c_lo + j, j):
                cp.start()

    def body(c, carry):
        slot = lax.rem(c - c_lo, SEG_BUFS)
        for cp in copies(c, slot):
            cp.wait()
        ahead = c + (SEG_BUFS - 1)

        @pl.when(ahead < c_hi)
        def _():
            for cp in copies(ahead, lax.rem(ahead - c_lo, SEG_BUFS)):
                cp.start()

        tok = tokbuf[slot]
        z = zbuf[slot]
        zh = z.astype(BF16)
        zl = (z - zh.astype(F32)).astype(BF16)
        row0 = c * SEG_ROWS
        u_first, u_end = jnp.int32(0), jnp.int32(0)
        for u in range(nsub):
            u_first = u_first + (offs_ref[blk0 + u + 1] <= row0).astype(jnp.int32)
            u_end = u_end + (offs_ref[blk0 + u] < row0 + SEG_ROWS).astype(jnp.int32)

        def add_block(u, valid):
            base = jnp.where(valid, (blk0 + u) * SEG_TOK, -2 * SEG_TOK)
            tid = base + lax.broadcasted_iota(jnp.int32, (SEG_TOK, SEG_ROWS), 0)
            onehot = jnp.where(tok == tid, 1.0, 0.0).astype(BF16)
            upd = (jnp.dot(onehot, zh, preferred_element_type=F32)
                   + jnp.dot(onehot, zl, preferred_element_type=F32))
            start = pl.multiple_of(jnp.minimum(u, nsub - 1) * SEG_TOK, SEG_TOK)
            o_ref[pl.ds(start, SEG_TOK), :] += upd

        add_block(u_first, u_first < u_end)
        add_block(u_first + 1, u_first + 1 < u_end)

        def more(u, cr):
            add_block(u, True)
            return cr

        lax.fori_loop(u_first + 2, u_end, more, 0)
        return carry

    lax.fori_loop(c_lo, c_hi, body, 0)


def _combine(offs, x, tok, z):
    n, d = x.shape
    tn = _pick(n, SEG_STEP_TOK)
    return pl.pallas_call(
        functools.partial(_combine_kernel, nsub=tn // SEG_TOK),
        grid_spec=pltpu.PrefetchScalarGridSpec(
            num_scalar_prefetch=1,
            grid=(n // tn,),
            in_specs=[pl.BlockSpec((tn, d), lambda i, offs: (i, 0)),
                      pl.BlockSpec(memory_space=pl.ANY), pl.BlockSpec(memory_space=pl.ANY)],
            out_specs=pl.BlockSpec((tn, d), lambda i, offs: (i, 0)),
            scratch_shapes=[pltpu.VMEM((SEG_BUFS, 1, SEG_ROWS), jnp.int32),
                            pltpu.VMEM((SEG_BUFS, SEG_ROWS, d), F32),
                            pltpu.SemaphoreType.DMA((2, SEG_BUFS))]),
        out_shape=jax.ShapeDtypeStruct((n, d), F32),
        compiler_params=_cparams(("arbitrary",)),
        name="combine_segsum",
    )(offs, x, tok, z)


def _block_diag(hd):
    i = jnp.arange(MXU_DIM)
    return (i[:, None] // hd == i[None, :] // hd).astype(BF16)


def _rope_tables_std(seq, dim):
    half = dim // 2
    inv_freq = ROPE_THETA ** (-jnp.arange(half, dtype=F32) * 2.0 / dim)
    ang = jnp.arange(seq).astype(F32)[:, None] * inv_freq[None, :]
    cos, sin = jnp.cos(ang), jnp.sin(ang)
    reps = LANES // dim
    return (jnp.tile(jnp.concatenate([cos, cos], axis=-1), (1, reps)),
            jnp.tile(jnp.concatenate([-sin, sin], axis=-1), (1, reps)))


def _rope_tables_axial(seq):
    sub = HEAD_DIM // 2
    half = sub // 2
    inv_freq = ROPE_THETA ** (-jnp.arange(half, dtype=F32) * 2.0 / sub)
    n_rows = seq // GRID_W
    rows = jnp.repeat(jnp.arange(n_rows), GRID_W).astype(F32)
    cols = jnp.tile(jnp.arange(GRID_W), n_rows).astype(F32)
    ar, ac = rows[:, None] * inv_freq[None, :], cols[:, None] * inv_freq[None, :]
    cos = jnp.concatenate([jnp.cos(ar), jnp.cos(ar), jnp.cos(ac), jnp.cos(ac)], axis=-1)
    sin = jnp.concatenate([-jnp.sin(ar), jnp.sin(ar), -jnp.sin(ac), jnp.sin(ac)], axis=-1)
    return jnp.tile(cos, (1, 2)), jnp.tile(sin, (1, 2))


def _dilated_bias(band):
    t = ATT_BLK
    rel = jnp.arange(-band - 1, band + 2)[:, None, None] * t
    d = rel + jnp.arange(t)[None, :, None] - jnp.arange(t)[None, None, :]
    cnt = jnp.zeros(d.shape, F32)
    for window, dil in DILATED_BRANCHES:
        cnt = cnt + ((d % dil == 0) & (jnp.abs(d) <= window // 2)).astype(F32)
    return jnp.where(cnt > 0, jnp.log2(jnp.maximum(cnt, 1.0)), NEG_BIG)


def _tile_gain(g, reps):
    return jnp.tile(g.astype(F32), reps)[None, :]


def _pick(n, pref):
    t = min(n, pref)
    while n % t:
        t //= 2
    return t


def _mixer_a(x, b, s, g, w_qkv, q_g, k_g):
    nq = A_HEADS * HEAD_DIM
    nkv = A_KV_HEADS * HEAD_DIM
    wq, wk, wv = w_qkv[:, :nq], w_qkv[:, nq:nq + nkv], w_qkv[:, nq + nkv:]
    wk2 = jnp.repeat(wk.reshape(D_MODEL, A_KV_HEADS, 1, HEAD_DIM), 2, axis=2).reshape(D_MODEL, -1)
    w = jnp.concatenate([wq, wk2], axis=1).astype(BF16)
    nk = 2 * nkv
    gain = jnp.concatenate([_tile_gain(q_g, nq // HEAD_DIM), _tile_gain(k_g, nk // HEAD_DIM)], axis=1)
    cos, sin = _rope_tables_axial(s)
    q, k, vt = _qkv_proj(x, g, w, wv.T.astype(BF16), gain, cos, sin, _block_diag(HEAD_DIM), nq=nq, nk=nk,
                         half=HEAD_DIM // 4, q_scale=HEAD_DIM ** -0.5 * LOG2E, v_heads=A_KV_HEADS, dv=HEAD_DIM,
                         seq=s, tm=_pick(s, 512))
    group_pairs = (A_HEADS // A_KV_HEADS) // 2
    return _attention(q.reshape(b, s, -1), k.reshape(b, s, -1), vt, mode="pair", qw=LANES, dv=HEAD_DIM, nv=1,
                      kmap=lambda p: p // group_pairs, vmap=lambda p: p // group_pairs, tq=_pick(s, ATT_TQ))


def _mixer_b(x, b, s, g, w_down, q_lat_g, kv_lat_g, w_uq, w_ukv, q_nope_g, q_rope_g, k_nope_g, k_rope_g):
    zpad = lambda a, n: jnp.concatenate([a, jnp.zeros(a.shape[:-1] + (n,), a.dtype)], axis=-1)
    w_kr = w_down[:, B_Q_LORA + B_KV_LORA:]
    wd = jnp.concatenate([w_down[:, :B_Q_LORA + B_KV_LORA], zpad(jnp.concatenate([w_kr, w_kr], axis=1), 2 * B_ROPE)],
                         axis=1).astype(BF16)
    uq = w_uq.reshape(B_Q_LORA, B_HEADS, B_NOPE + B_ROPE)
    uq_n = uq[:, :, :B_NOPE].reshape(B_Q_LORA, -1)
    uq_r = zpad(uq[:, :, B_NOPE:].reshape(B_Q_LORA, B_HEADS // 2, 2 * B_ROPE), LANES - 2 * B_ROPE).reshape(B_Q_LORA, -1)
    wuq = jnp.concatenate([uq_n, uq_r], axis=1).astype(BF16)
    ukv = w_ukv.reshape(B_KV_LORA, B_HEADS, B_NOPE + B_V)
    wuk = ukv[:, :, :B_NOPE].reshape(B_KV_LORA, -1).astype(BF16)
    wuvt = ukv[:, :, B_NOPE:].reshape(B_KV_LORA, -1).T.astype(BF16)
    gq = jnp.concatenate([_tile_gain(q_nope_g, B_HEADS), _tile_gain(q_rope_g, B_HEADS * B_NOPE // B_ROPE)], axis=1)
    gk = jnp.concatenate([_tile_gain(k_nope_g, B_HEADS), _tile_gain(k_rope_g, LANES // B_ROPE)], axis=1)
    cos, sin = _rope_tables_std(s, B_ROPE)
    q, k, vt = _mla_proj(x, g, wd, q_lat_g.astype(F32)[None, :], kv_lat_g.astype(F32)[None, :], wuq, wuk, wuvt,
                         _block_diag(B_NOPE), _block_diag(B_ROPE), gq, gk, cos, sin,
                         scale=(B_NOPE + B_ROPE) ** -0.5 * LOG2E, seq=s, tm=_pick(s, 512))
    return _attention(q.reshape(b, s, -1), k.reshape(b, s, -1), vt, mode="pair", qw=2 * LANES, dv=B_V, nv=2,
                      kmap=lambda p: p, vmap=lambda p: p, tq=_pick(s, ATT_TQ))


def _mixer_c(x, b, s, g, w_qkv, q_g, k_g, lq1, lk1, lq2, lk2, subln_g, layer_idx):
    nq = 2 * C_HEADS * HEAD_DIM
    gain = jnp.concatenate([_tile_gain(q_g, nq // HEAD_DIM), _tile_gain(k_g, nq // HEAD_DIM)], axis=1)
    cos, sin = _rope_tables_std(s, HEAD_DIM)
    q, k, vt = _qkv_proj(x, g, w_qkv[:, :2 * nq].astype(BF16), w_qkv[:, 2 * nq:].T.astype(BF16), gain, cos, sin,
                         _block_diag(HEAD_DIM), nq=nq, nk=nq, half=HEAD_DIM // 2, q_scale=HEAD_DIM ** -0.5 * LOG2E,
                         v_heads=C_HEADS, dv=2 * HEAD_DIM, seq=s, tm=_pick(s, 512))
    lam_init = LAMBDA_INIT_BASE - LAMBDA_INIT_AMP * math.exp(-LAMBDA_INIT_RATE * layer_idx)
    lam = (jnp.exp(jnp.sum(lq1.astype(F32) * lk1.astype(F32))) - jnp.exp(jnp.sum(lq2.astype(F32) * lk2.astype(F32)))
           + lam_init)
    lam_arr = jnp.stack([lam, jnp.asarray(lam_init, F32)]).reshape(1, 2).astype(F32)
    return _attention(q.reshape(b, s, -1), k.reshape(b, s, -1), vt, mode="diff", qw=LANES, dv=2 * HEAD_DIM, nv=1,
                      kmap=lambda p: p, vmap=lambda p: p, tq=_pick(s, ATT_TQ), lam=lam_arr,
                      subg=subln_g.astype(F32)[None, :])


def _mixer_d(x, b, s, g, w_qkv, q_g, k_g):
    nq = D_HEADS * HEAD_DIM
    gain = jnp.concatenate([_tile_gain(q_g, D_HEADS), _tile_gain(k_g, D_HEADS)], axis=1)
    cos, sin = _rope_tables_std(s, HEAD_DIM)
    q, k, vt = _qkv_proj(x, g, w_qkv[:, :2 * nq].astype(BF16), w_qkv[:, 2 * nq:].T.astype(BF16), gain, cos, sin,
                         _block_diag(HEAD_DIM), nq=nq, nk=nq, half=HEAD_DIM // 2, q_scale=HEAD_DIM ** -0.5 * LOG2E,
                         v_heads=D_HEADS, dv=HEAD_DIM, seq=s, tm=_pick(s, 512))
    reach = max(w // 2 for w, _ in DILATED_BRANCHES)
    band = -(-reach // ATT_BLK)
    tq = _pick(s, ATT_TQ if ATT_TQ // ATT_BLK + 2 * band >= s // ATT_BLK else 2 * ATT_BLK)
    return _attention(q.reshape(b, s, -1), k.reshape(b, s, -1), vt, mode="pair", qw=LANES, dv=HEAD_DIM, nv=2,
                      kmap=lambda p: p, vmap=lambda p: p, tq=tq, bias=_dilated_bias(band), band=band)


def _ec_ffn(xnew, xn, aff, wg, wu, wd, layer):
    n, d = xnew.shape
    cap = (EC_CAPACITY * n) // N_EXPERTS
    gate, idx = lax.top_k(aff, cap)
    y = _expert_ffn(xn[idx], wg, wu, wd, gate[..., None], layer=layer, tm=_pick(cap, 512))
    flat = idx.reshape(-1)
    tok_sorted, order = lax.sort_key_val(flat, jnp.arange(flat.size, dtype=jnp.int32))
    bounds = jnp.arange(0, n + 1, SEG_TOK, dtype=jnp.int32)
    offs = jnp.sum((tok_sorted[None, :] < bounds[:, None]).astype(jnp.int32), axis=1)
    return _combine(offs, xnew, tok_sorted.reshape(-1, 1, SEG_ROWS), y.reshape(-1, d)[order])


def kernel(x_prompt, x_sample, norm_mix_g, norm_ffn_g, a_w_qkv, a_q_norm_g, a_k_norm_g, a_w_o, b_w_down, b_q_lat_norm_g, b_kv_lat_norm_g, b_w_uq, b_w_ukv, b_q_nope_norm_g, b_q_rope_norm_g, b_k_nope_norm_g, b_k_rope_norm_g, b_w_o, c_w_qkv, c_q_norm_g, c_k_norm_g, c_lambda_q1, c_lambda_k1, c_lambda_q2, c_lambda_k2, c_subln_g, c_w_o, d_w_qkv, d_q_norm_g, d_k_norm_g, d_w_o, ec_w_router, ec_w_gate, ec_w_up, ec_w_down):
    depth = norm_mix_g.shape[0]
    wg_all, wu_all, wd_all = ec_w_gate.astype(BF16), ec_w_up.astype(BF16), ec_w_down.astype(BF16)

    def trunk(x3):
        b, s, d = x3.shape
        x = x3.reshape(b * s, d)
        for i in range(depth):
            m, j = i % N_MIXERS, i // N_MIXERS
            g = norm_mix_g[i].astype(F32)[None, :]
            if m == 0:
                o, w_o = _mixer_a(x, b, s, g, a_w_qkv[j], a_q_norm_g[j], a_k_norm_g[j]), a_w_o[j]
            elif m == 1:
                o = _mixer_b(x, b, s, g, b_w_down[j], b_q_lat_norm_g[j], b_kv_lat_norm_g[j], b_w_uq[j], b_w_ukv[j],
                             b_q_nope_norm_g[j], b_q_rope_norm_g[j], b_k_nope_norm_g[j], b_k_rope_norm_g[j])
                w_o = b_w_o[j]
            elif m == 2:
                o = _mixer_c(x, b, s, g, c_w_qkv[j], c_q_norm_g[j], c_k_norm_g[j], c_lambda_q1[j], c_lambda_k1[j],
                             c_lambda_q2[j], c_lambda_k2[j], c_subln_g[j], i)
                w_o = c_w_o[j]
            else:
                o, w_o = _mixer_d(x, b, s, g, d_w_qkv[j], d_q_norm_g[j], d_k_norm_g[j]), d_w_o[j]
            xnew, xn, aff = _oproj(o.reshape(b * s, -1), w_o.astype(BF16), x, norm_ffn_g[i].astype(F32)[None, :],
                                   ec_w_router[i].T.astype(BF16), tm=_pick(b * s, 512))
            x = _ec_ffn(xnew, xn, aff, wg_all, wu_all, wd_all, i)
        return x.reshape(b, s, d)

    return (trunk(x_prompt), trunk(x_sample))
```

```python
import functools
import math

import jax
import jax.numpy as jnp
from jax import lax
from jax.experimental import pallas as pl
from jax.experimental.pallas import tpu as pltpu

F32 = jnp.float32
BF16 = jnp.bfloat16

D_MODEL = 1024
HEAD_DIM = 64
GRID_W = 64
ROPE_THETA = 10000.0
RMS_EPS = 1e-6
N_MIXERS = 4
A_HEADS, A_KV_HEADS = 16, 4
B_HEADS, B_Q_LORA, B_KV_LORA, B_NOPE, B_ROPE, B_V = 16, 384, 256, 64, 32, 64
C_HEADS = 8
LAMBDA_INIT_BASE, LAMBDA_INIT_AMP, LAMBDA_INIT_RATE = 0.8, 0.6, 0.3
D_HEADS = 16
DILATED_BRANCHES = ((128, 1), (512, 4), (2048, 16))
N_EXPERTS = 16
EC_CAPACITY = 2
EXPERT_FF = 2 * D_MODEL

LANES = 128
MXU_DIM = 256
VMEM_LIMIT = 56 * 1024 * 1024
V_ONES = 16
ATT_BLK = MXU_DIM
ATT_TQ = 8 * ATT_BLK
SEG_TOK = MXU_DIM
SEG_ROWS_LOG2 = 8
SEG_ROWS = 1 << SEG_ROWS_LOG2
SEG_STEP_TOK = 8 * SEG_TOK
SEG_BUFS = 4
NEG_BIG = -1e30
LOG2E = math.log2(math.e)


def _cparams(sem):
    return pltpu.CompilerParams(dimension_semantics=sem, vmem_limit_bytes=VMEM_LIMIT)


_NT = (((1,), (1,)), ((), ()))


def _rms_rows(x, g):
    return x * lax.rsqrt(jnp.mean(x * x, axis=-1, keepdims=True) + RMS_EPS) * g


def _head_rms(t, bd, gain, hd):
    outs = []
    for j in range(t.shape[1] // MXU_DIM):
        tj = t[:, j * MXU_DIM:(j + 1) * MXU_DIM]
        ss = jnp.dot((tj * tj).astype(BF16), bd, preferred_element_type=F32)
        outs.append(tj * lax.rsqrt(ss * (1.0 / hd) + RMS_EPS))
    y = outs[0] if len(outs) == 1 else jnp.concatenate(outs, axis=-1)
    return y * gain


def _rope_slabs(y, cos, sin, half):
    lane = lax.broadcasted_iota(jnp.int32, (1, LANES), 1)
    first = (lane % (2 * half)) < half
    outs = []
    for j in range(y.shape[1] // LANES):
        yj = y[:, j * LANES:(j + 1) * LANES]
        swapped = jnp.where(first, pltpu.roll(yj, LANES - half, 1), pltpu.roll(yj, half, 1))
        outs.append(yj * cos + swapped * sin)
    return outs[0] if len(outs) == 1 else jnp.concatenate(outs, axis=-1)


def _store_vt(vt_ref, vt, *, heads, dv):
    hv = dv + V_ONES
    ones = jnp.ones((V_ONES, ATT_BLK), BF16)
    vt = vt.astype(BF16)
    for j in range(vt.shape[1] // ATT_BLK):
        for h in range(heads):
            vt_ref[0, j, h * hv:h * hv + dv, :] = vt[h * dv:(h + 1) * dv, j * ATT_BLK:(j + 1) * ATT_BLK]
            vt_ref[0, j, h * hv + dv:(h + 1) * hv, :] = ones


def _qkv_kernel(x_ref, g_ref, w_ref, wvt_ref, bd_ref, gain_ref, cos_ref, sin_ref, q_ref, k_ref, vt_ref,
                *, nq, half, q_scale, v_heads, dv):
    xn = _rms_rows(x_ref[...], g_ref[...]).astype(BF16)
    y = jnp.dot(xn, w_ref[...], preferred_element_type=F32)
    bd = bd_ref[...]
    cos, sin = cos_ref[...], sin_ref[...]
    gain = gain_ref[...]
    q = _rope_slabs(_head_rms(y[:, :nq], bd, gain[:, :nq], HEAD_DIM), cos, sin, half) * q_scale
    k = _rope_slabs(_head_rms(y[:, nq:], bd, gain[:, nq:], HEAD_DIM), cos, sin, half)
    q_ref[...] = q.astype(BF16)
    k_ref[...] = k.astype(BF16)
    vt = lax.dot_general(wvt_ref[...], xn, _NT, preferred_element_type=F32)
    _store_vt(vt_ref, vt, heads=v_heads, dv=dv)


def _vt_out(n, seq, tm, vr):
    nt = seq // tm
    spec = pl.BlockSpec((1, tm // ATT_BLK, vr, ATT_BLK), lambda i: (i // nt, i % nt, 0, 0))
    return spec, jax.ShapeDtypeStruct((n // seq, seq // ATT_BLK, vr, ATT_BLK), BF16)


def _qkv_proj(x, g, w, wvt, gain, cos, sin, bd, *, nq, nk, half, q_scale, v_heads, dv, seq, tm):
    n, d = x.shape
    nt = seq // tm
    row = lambda i: (i, 0)
    const = lambda i: (0, 0)
    vt_spec, vt_shape = _vt_out(n, seq, tm, v_heads * (dv + V_ONES))
    return pl.pallas_call(
        functools.partial(_qkv_kernel, nq=nq, half=half, q_scale=q_scale, v_heads=v_heads, dv=dv),
        grid=(n // tm,),
        in_specs=[pl.BlockSpec((tm, d), row), pl.BlockSpec((1, d), const),
                  pl.BlockSpec(w.shape, const), pl.BlockSpec(wvt.shape, const), pl.BlockSpec(bd.shape, const),
                  pl.BlockSpec(gain.shape, const),
                  pl.BlockSpec((tm, LANES), lambda i: (i % nt, 0)),
                  pl.BlockSpec((tm, LANES), lambda i: (i % nt, 0))],
        out_specs=[pl.BlockSpec((tm, nq), row), pl.BlockSpec((tm, nk), row), vt_spec],
        out_shape=[jax.ShapeDtypeStruct((n, nq), BF16), jax.ShapeDtypeStruct((n, nk), BF16), vt_shape],
        compiler_params=_cparams(("parallel",)),
        name="qkv_proj",
    )(x, g, w, wvt, bd, gain, cos, sin)


def _mla_kernel(x_ref, g_ref, wd_ref, qlg_ref, kvlg_ref, wuq_ref, wuk_ref, wuvt_ref, bd64_ref, bd32_ref,
                gq_ref, gk_ref, cos_ref, sin_ref, q_ref, k_ref, vt_ref, *, scale):
    xn = _rms_rows(x_ref[...], g_ref[...]).astype(BF16)
    c = jnp.dot(xn, wd_ref[...], preferred_element_type=F32)
    cq = _rms_rows(c[:, :B_Q_LORA], qlg_ref[...]).astype(BF16)
    ckv = _rms_rows(c[:, B_Q_LORA:B_Q_LORA + B_KV_LORA], kvlg_ref[...]).astype(BF16)
    q = jnp.dot(cq, wuq_ref[...], preferred_element_type=F32)
    kn = jnp.dot(ckv, wuk_ref[...], preferred_element_type=F32)
    bd64, bd32 = bd64_ref[...], bd32_ref[...]
    cos, sin = cos_ref[...], sin_ref[...]
    gq, gk = gq_ref[...], gk_ref[...]
    hw = B_HEADS * B_NOPE
    qn = _head_rms(q[:, :hw], bd64, gq[:, :hw], B_NOPE) * scale
    qr = _rope_slabs(_head_rms(q[:, hw:], bd32, gq[:, hw:], B_ROPE), cos, sin, B_ROPE // 2) * scale
    kn = _head_rms(kn, bd64, gk[:, :hw], B_NOPE)
    kr_raw = c[:, B_Q_LORA + B_KV_LORA:]
    kr2 = _head_rms(jnp.concatenate([kr_raw, kr_raw], axis=-1), bd32,
                    jnp.concatenate([gk[:, hw:], gk[:, hw:]], axis=-1), B_ROPE)
    kr = _rope_slabs(kr2[:, :LANES], cos, sin, B_ROPE // 2).astype(BF16)
    qn, qr, kn = qn.astype(BF16), qr.astype(BF16), kn.astype(BF16)
    for p in range(B_HEADS // 2):
        sl = slice(p * LANES, (p + 1) * LANES)
        q_ref[:, 2 * p * LANES:(2 * p + 1) * LANES] = qn[:, sl]
        q_ref[:, (2 * p + 1) * LANES:(2 * p + 2) * LANES] = qr[:, sl]
        k_ref[:, 2 * p * LANES:(2 * p + 1) * LANES] = kn[:, sl]
        k_ref[:, (2 * p + 1) * LANES:(2 * p + 2) * LANES] = kr
    vt = lax.dot_general(wuvt_ref[...], ckv, _NT, preferred_element_type=F32)
    _store_vt(vt_ref, vt, heads=B_HEADS, dv=B_V)


def _mla_proj(x, g, wd, qlg, kvlg, wuq, wuk, wuvt, bd64, bd32, gq, gk, cos, sin, *, scale, seq, tm):
    n, d = x.shape
    nt = seq // tm
    row = lambda i: (i, 0)
    const = lambda i: (0, 0)
    full = lambda a: pl.BlockSpec(a.shape, const)
    wq = (B_HEADS // 2) * 2 * LANES
    vt_spec, vt_shape = _vt_out(n, seq, tm, B_HEADS * (B_V + V_ONES))
    return pl.pallas_call(
        functools.partial(_mla_kernel, scale=scale),
        grid=(n // tm,),
        in_specs=[pl.BlockSpec((tm, d), row), full(g), full(wd), full(qlg), full(kvlg), full(wuq), full(wuk),
                  full(wuvt), full(bd64), full(bd32), full(gq), full(gk),
                  pl.BlockSpec((tm, LANES), lambda i: (i % nt, 0)),
                  pl.BlockSpec((tm, LANES), lambda i: (i % nt, 0))],
        out_specs=[pl.BlockSpec((tm, wq), row), pl.BlockSpec((tm, wq), row), vt_spec],
        out_shape=[jax.ShapeDtypeStruct((n, wq), BF16), jax.ShapeDtypeStruct((n, wq), BF16), vt_shape],
        compiler_params=_cparams(("parallel",)),
        name="mla_proj",
    )(x, g, wd, qlg, kvlg, wuq, wuk, wuvt, bd64, bd32, gq, gk, cos, sin)


def _attn_kernel(*refs, mode, qw, dv, nv, tq, nk, band):
    it = iter(refs)
    q_ref, k_ref, vt_ref = next(it), next(it), next(it)
    bias_ref = next(it) if band is not None else None
    if mode == "diff":
        lam_ref, subg_ref = next(it), next(it)
    o_ref = next(it)
    qm_sc, m_sc, acc_sc, sa_sc, sb_sc = next(it), next(it), next(it), next(it), next(it)

    blk = ATT_BLK
    hv = dv + V_ONES
    nsub = tq // blk
    chains = [(r, h) for r in range(nsub) for h in range(2)]
    qi = pl.program_id(2)

    lane = lax.broadcasted_iota(jnp.int32, (1, qw), 1)
    if qw == LANES:
        masks = (lane < HEAD_DIM, lane >= HEAD_DIM)
    else:
        masks = ((lane < HEAD_DIM) | ((lane >= LANES) & (lane < LANES + B_ROPE)),
                 ((lane >= HEAD_DIM) & (lane < LANES)) | ((lane >= LANES + B_ROPE) & (lane < LANES + 2 * B_ROPE)))
    for ci, (r, h) in enumerate(chains):
        qr = q_ref[0, r * blk:(r + 1) * blk, :]
        qm_sc[ci] = jnp.where(masks[h], qr, jnp.zeros_like(qr))
    m_sc[...] = jnp.full(m_sc.shape, NEG_BIG, F32)
    acc_sc[...] = jnp.zeros(acc_sc.shape, F32)

    def score_chain(ci, c, k, s_sc):
        st = lax.dot_general(k, qm_sc[ci], _NT, preferred_element_type=F32)
        if band is not None:
            rel = c - (qi * nsub + chains[ci][0]) + band + 1
            st = st + bias_ref[jnp.clip(rel, 0, 2 * band + 2)]
        s_sc[ci] = st

    def consume_chain(ci, vt, s_sc):
        st = s_sc[ci]
        m_old = m_sc[ci]
        m_new = jnp.maximum(m_old, jnp.max(st, axis=0, keepdims=True))
        p = jnp.exp2(st - m_new).astype(BF16)
        alpha = jnp.exp2(m_old - m_new)
        m_sc[ci] = m_new
        voff = (chains[ci][1] if nv == 2 else 0) * hv
        pv = jnp.dot(vt[voff:voff + hv], p, preferred_element_type=F32)
        acc_sc[ci] = acc_sc[ci] * alpha + pv

    def step(c_cur, s_cur, c_next=None, s_next=None):
        vt = vt_ref[0, c_cur]
        if c_next is not None:
            k = k_ref[0, pl.ds(pl.multiple_of(c_next * blk, blk), blk), :]
        for ci in range(len(chains)):
            if c_next is not None:
                score_chain(ci, c_next, k, s_next)
            consume_chain(ci, vt, s_cur)

    if band is None:
        lo, hi = 0, nk
    else:
        lo, hi = jnp.maximum(qi * nsub - band, 0), jnp.minimum((qi + 1) * nsub + band, nk)
    npairs = (hi - lo - 1) // 2

    def body(i, carry):
        c = lo + 2 * i
        step(c, sa_sc, c + 1, sb_sc)
        step(c + 1, sb_sc, c + 2, sa_sc)
        return carry

    k0 = k_ref[0, pl.ds(pl.multiple_of(lo * blk, blk), blk), :]
    for ci in range(len(chains)):
        score_chain(ci, lo, k0, sa_sc)
    lax.fori_loop(0, npairs, body, 0)
    c_tail = lo + 2 * npairs
    if band is None:
        if (hi - lo) % 2 == 0:
            step(c_tail, sa_sc, c_tail + 1, sb_sc)
            step(c_tail + 1, sb_sc)
        else:
            step(c_tail, sa_sc)
    else:
        two_left = (hi - c_tail) == 2

        @pl.when(two_left)
        def _():
            step(c_tail, sa_sc, c_tail + 1, sb_sc)
            step(c_tail + 1, sb_sc)

        @pl.when(jnp.logical_not(two_left))
        def _():
            step(c_tail, sa_sc)

    for r in range(nsub):
        a, b = acc_sc[2 * r], acc_sc[2 * r + 1]
        oa = a[:dv] / a[dv:dv + 1]
        ob = b[:dv] / b[dv:dv + 1]
        if mode == "pair":
            o = jnp.concatenate([oa, ob], axis=0).T
        else:
            o = (oa - lam_ref[0, 0] * ob).T
            o = _rms_rows(o, subg_ref[...]) * (1.0 - lam_ref[0, 1])
        o_ref[0, r * blk:(r + 1) * blk, :] = o.astype(o_ref.dtype)


def _attention(q, k, vt, *, mode, qw, dv, nv, kmap, vmap, tq, bias=None, band=None, lam=None, subg=None):
    b, s, cq = q.shape
    pairs = cq // qw
    nk = s // ATT_BLK
    hv = dv + V_ONES
    in_specs = [pl.BlockSpec((1, tq, qw), lambda bi, p, i: (bi, i, p)),
                pl.BlockSpec((1, s, qw), lambda bi, p, i: (bi, 0, kmap(p))),
                pl.BlockSpec((1, nk, nv * hv, ATT_BLK), lambda bi, p, i: (bi, 0, vmap(p), 0))]
    args = [q, k, vt]
    if band is not None:
        in_specs.append(pl.BlockSpec(bias.shape, lambda bi, p, i: (0, 0, 0)))
        args.append(bias)
    if mode == "diff":
        in_specs.append(pl.BlockSpec(memory_space=pltpu.SMEM))
        in_specs.append(pl.BlockSpec(subg.shape, lambda bi, p, i: (0, 0)))
        args += [lam, subg]
    nchains = 2 * (tq // ATT_BLK)
    return pl.pallas_call(
        functools.partial(_attn_kernel, mode=mode, qw=qw, dv=dv, nv=nv, tq=tq, nk=nk, band=band),
        grid=(b, pairs, s // tq),
        in_specs=in_specs,
        out_specs=pl.BlockSpec((1, tq, LANES), lambda bi, p, i: (bi, i, p)),
        out_shape=jax.ShapeDtypeStruct((b, s, pairs * LANES), BF16),
        scratch_shapes=[pltpu.VMEM((nchains, ATT_BLK, qw), BF16), pltpu.VMEM((nchains, 1, ATT_BLK), F32),
                        pltpu.VMEM((nchains, hv, ATT_BLK), F32),
                        pltpu.VMEM((nchains, ATT_BLK, ATT_BLK), F32), pltpu.VMEM((nchains, ATT_BLK, ATT_BLK), F32)],
        compiler_params=_cparams(("parallel", "parallel", "arbitrary")),
        name="attention_" + mode,
    )(*args)


def _oproj_kernel(o_ref, w_ref, x_ref, g_ref, wr_ref, xnew_ref, xn_ref, aff_ref):
    xnew = x_ref[...] + jnp.dot(o_ref[...], w_ref[...], preferred_element_type=F32)
    xnew_ref[...] = xnew
    xn = _rms_rows(xnew, g_ref[...])
    xn_ref[...] = xn
    logits = lax.dot_general(wr_ref[...], xn.astype(BF16), _NT, preferred_element_type=F32)
    z = logits - jnp.max(logits, axis=0, keepdims=True)
    e = jnp.exp(z)
    aff_ref[...] = e / jnp.sum(e, axis=0, keepdims=True)


def _oproj(o, w, x, g, wr_t, *, tm):
    n, d = x.shape
    row = lambda i: (i, 0)
    const = lambda i: (0, 0)
    return pl.pallas_call(
        _oproj_kernel,
        grid=(n // tm,),
        in_specs=[pl.BlockSpec((tm, o.shape[1]), row), pl.BlockSpec(w.shape, const), pl.BlockSpec((tm, d), row),
                  pl.BlockSpec((1, d), const), pl.BlockSpec(wr_t.shape, const)],
        out_specs=[pl.BlockSpec((tm, d), row), pl.BlockSpec((tm, d), row),
                   pl.BlockSpec((N_EXPERTS, tm), lambda i: (0, i))],
        out_shape=[jax.ShapeDtypeStruct((n, d), F32), jax.ShapeDtypeStruct((n, d), F32),
                   jax.ShapeDtypeStruct((N_EXPERTS, n), F32)],
        compiler_params=_cparams(("parallel",)),
        name="oproj_router",
    )(o, w, x, g, wr_t)


def _ffn_kernel(xg_ref, wg_ref, wu_ref, wd_ref, gate_ref, y_ref):
    xg = xg_ref[0].astype(BF16)
    a = jnp.dot(xg, wg_ref[0, 0], preferred_element_type=F32)
    u = jnp.dot(xg, wu_ref[0, 0], preferred_element_type=F32)
    h = (a * jax.nn.sigmoid(a) * u).astype(BF16)
    y_ref[0] = jnp.dot(h, wd_ref[0, 0], preferred_element_type=F32) * gate_ref[0]


def _expert_ffn(xg, wg, wu, wd, gate, *, layer, tm):
    e, cap, d = xg.shape
    ff = wg.shape[3]
    return pl.pallas_call(
        _ffn_kernel,
        grid=(e, cap // tm),
        in_specs=[pl.BlockSpec((1, tm, d), lambda ei, i: (ei, i, 0)),
                  pl.BlockSpec((1, 1, d, ff), lambda ei, i: (layer, ei, 0, 0)),
                  pl.BlockSpec((1, 1, d, ff), lambda ei, i: (layer, ei, 0, 0)),
                  pl.BlockSpec((1, 1, ff, d), lambda ei, i: (layer, ei, 0, 0)),
                  pl.BlockSpec((1, tm, 1), lambda ei, i: (ei, i, 0))],
        out_specs=pl.BlockSpec((1, tm, d), lambda ei, i: (ei, i, 0)),
        out_shape=jax.ShapeDtypeStruct((e, cap, d), F32),
        compiler_params=_cparams(("parallel", "arbitrary")),
        name="expert_ffn",
    )(xg, wg, wu, wd, gate)


def _combine_kernel(offs_ref, x_ref, tok_hbm, z_hbm, o_ref, tokbuf, zbuf, sem, *, nsub):
    step = pl.program_id(0)
    blk0 = step * nsub
    c_lo = lax.shift_right_logical(offs_ref[blk0], SEG_ROWS_LOG2)
    c_hi = lax.shift_right_logical(offs_ref[blk0 + nsub] + (SEG_ROWS - 1), SEG_ROWS_LOG2)
    o_ref[...] = x_ref[...]

    def copies(c, slot):
        return (pltpu.make_async_copy(tok_hbm.at[c], tokbuf.at[slot], sem.at[0, slot]),
                pltpu.make_async_copy(z_hbm.at[pl.ds(pl.multiple_of(c * SEG_ROWS, SEG_ROWS), SEG_ROWS)],
                                      zbuf.at[slot], sem.at[1, slot]))

    for j in range(SEG_BUFS - 1):
        @pl.when(c_lo + j < c_hi)
        def _():
            for cp in copies(c_lo + j, j):
                cp.start()

    def body(c, carry):
        slot = lax.rem(c - c_lo, SEG_BUFS)
        for cp in copies(c, slot):
            cp.wait()
        ahead = c + (SEG_BUFS - 1)

        @pl.when(ahead < c_hi)
        def _():
            for cp in copies(ahead, lax.rem(ahead - c_lo, SEG_BUFS)):
                cp.start()

        tok = tokbuf[slot]
        z = zbuf[slot]
        zh = z.astype(BF16)
        zl = (z - zh.astype(F32)).astype(BF16)
        row0 = c * SEG_ROWS
        u_first, u_end = jnp.int32(0), jnp.int32(0)
        for u in range(nsub):
            u_first = u_first + (offs_ref[blk0 + u + 1] <= row0).astype(jnp.int32)
            u_end = u_end + (offs_ref[blk0 + u] < row0 + SEG_ROWS).astype(jnp.int32)

        def add_block(u, valid):
            base = jnp.where(valid, (blk0 + u) * SEG_TOK, -2 * SEG_TOK)
            tid = base + lax.broadcasted_iota(jnp.int32, (SEG_TOK, SEG_ROWS), 0)
            onehot = jnp.where(tok == tid, 1.0, 0.0).astype(BF16)
            upd = (jnp.dot(onehot, zh, preferred_element_type=F32)
                   + jnp.dot(onehot, zl, preferred_element_type=F32))
            start = pl.multiple_of(jnp.minimum(u, nsub - 1) * SEG_TOK, SEG_TOK)
            o_ref[pl.ds(start, SEG_TOK), :] += upd

        add_block(u_first, u_first < u_end)
        add_block(u_first + 1, u_first + 1 < u_end)

        def more(u, cr):
            add_block(u, True)
            return cr

        lax.fori_loop(u_first + 2, u_end, more, 0)
        return carry

    lax.fori_loop(c_lo, c_hi, body, 0)


def _combine(offs, x, tok, z):
    n, d = x.shape
    tn = _pick(n, SEG_STEP_TOK)
    return pl.pallas_call(
        functools.partial(_combine_kernel, nsub=tn // SEG_TOK),
        grid_spec=pltpu.PrefetchScalarGridSpec(
            num_scalar_prefetch=1,
            grid=(n // tn,),
            in_specs=[pl.BlockSpec((tn, d), lambda i, offs: (i, 0)),
                      pl.BlockSpec(memory_space=pl.ANY), pl.BlockSpec(memory_space=pl.ANY)],
            out_specs=pl.BlockSpec((tn, d), lambda i, offs: (i, 0)),
            scratch_shapes=[pltpu.VMEM((SEG_BUFS, 1, SEG_ROWS), jnp.int32),
                            pltpu.VMEM((SEG_BUFS, SEG_ROWS, d), F32),
                            pltpu.SemaphoreType.DMA((2, SEG_BUFS))]),
        out_shape=jax.ShapeDtypeStruct((n, d), F32),
        compiler_params=_cparams(("arbitrary",)),
        name="combine_segsum",
    )(offs, x, tok, z)


def _block_diag(hd):
    i = jnp.arange(MXU_DIM)
    return (i[:, None] // hd == i[None, :] // hd).astype(BF16)


def _rope_tables_std(seq, dim):
    half = dim // 2
    inv_freq = ROPE_THETA ** (-jnp.arange(half, dtype=F32) * 2.0 / dim)
    ang = jnp.arange(seq).astype(F32)[:, None] * inv_freq[None, :]
    cos, sin = jnp.cos(ang), jnp.sin(ang)
    reps = LANES // dim
    return (jnp.tile(jnp.concatenate([cos, cos], axis=-1), (1, reps)),
            jnp.tile(jnp.concatenate([-sin, sin], axis=-1), (1, reps)))


def _rope_tables_axial(seq):
    sub = HEAD_DIM // 2
    half = sub // 2
    inv_freq = ROPE_THETA ** (-jnp.arange(half, dtype=F32) * 2.0 / sub)
    n_rows = seq // GRID_W
    rows = jnp.repeat(jnp.arange(n_rows), GRID_W).astype(F32)
    cols = jnp.tile(jnp.arange(GRID_W), n_rows).astype(F32)
    ar, ac = rows[:, None] * inv_freq[None, :], cols[:, None] * inv_freq[None, :]
    cos = jnp.concatenate([jnp.cos(ar), jnp.cos(ar), jnp.cos(ac), jnp.cos(ac)], axis=-1)
    sin = jnp.concatenate([-jnp.sin(ar), jnp.sin(ar), -jnp.sin(ac), jnp.sin(ac)], axis=-1)
    return jnp.tile(cos, (1, 2)), jnp.tile(sin, (1, 2))


def _dilated_bias(band):
    t = ATT_BLK
    rel = jnp.arange(-band - 1, band + 2)[:, None, None] * t
    d = rel + jnp.arange(t)[None, :, None] - jnp.arange(t)[None, None, :]
    cnt = jnp.zeros(d.shape, F32)
    for window, dil in DILATED_BRANCHES:
        cnt = cnt + ((d % dil == 0) & (jnp.abs(d) <= window // 2)).astype(F32)
    return jnp.where(cnt > 0, jnp.log2(jnp.maximum(cnt, 1.0)), NEG_BIG)


def _tile_gain(g, reps):
    return jnp.tile(g.astype(F32), reps)[None, :]


def _pick(n, pref):
    t = min(n, pref)
    while n % t:
        t //= 2
    return t


def _mixer_a(x, b, s, g, w_qkv, q_g, k_g):
    nq = A_HEADS * HEAD_DIM
    nkv = A_KV_HEADS * HEAD_DIM
    wq, wk, wv = w_qkv[:, :nq], w_qkv[:, nq:nq + nkv], w_qkv[:, nq + nkv:]
    wk2 = jnp.repeat(wk.reshape(D_MODEL, A_KV_HEADS, 1, HEAD_DIM), 2, axis=2).reshape(D_MODEL, -1)
    w = jnp.concatenate([wq, wk2], axis=1).astype(BF16)
    nk = 2 * nkv
    gain = jnp.concatenate([_tile_gain(q_g, nq // HEAD_DIM), _tile_gain(k_g, nk // HEAD_DIM)], axis=1)
    cos, sin = _rope_tables_axial(s)
    q, k, vt = _qkv_proj(x, g, w, wv.T.astype(BF16), gain, cos, sin, _block_diag(HEAD_DIM), nq=nq, nk=nk,
                         half=HEAD_DIM // 4, q_scale=HEAD_DIM ** -0.5 * LOG2E, v_heads=A_KV_HEADS, dv=HEAD_DIM,
                         seq=s, tm=_pick(s, 512))
    group_pairs = (A_HEADS // A_KV_HEADS) // 2
    return _attention(q.reshape(b, s, -1), k.reshape(b, s, -1), vt, mode="pair", qw=LANES, dv=HEAD_DIM, nv=1,
                      kmap=lambda p: p // group_pairs, vmap=lambda p: p // group_pairs, tq=_pick(s, ATT_TQ))


def _mixer_b(x, b, s, g, w_down, q_lat_g, kv_lat_g, w_uq, w_ukv, q_nope_g, q_rope_g, k_nope_g, k_rope_g):
    zpad = lambda a, n: jnp.concatenate([a, jnp.zeros(a.shape[:-1] + (n,), a.dtype)], axis=-1)
    w_kr = w_down[:, B_Q_LORA + B_KV_LORA:]
    wd = jnp.concatenate([w_down[:, :B_Q_LORA + B_KV_LORA], zpad(jnp.concatenate([w_kr, w_kr], axis=1), 2 * B_ROPE)],
                         axis=1).astype(BF16)
    uq = w_uq.reshape(B_Q_LORA, B_HEADS, B_NOPE + B_ROPE)
    uq_n = uq[:, :, :B_NOPE].reshape(B_Q_LORA, -1)
    uq_r = zpad(uq[:, :, B_NOPE:].reshape(B_Q_LORA, B_HEADS // 2, 2 * B_ROPE), LANES - 2 * B_ROPE).reshape(B_Q_LORA, -1)
    wuq = jnp.concatenate([uq_n, uq_r], axis=1).astype(BF16)
    ukv = w_ukv.reshape(B_KV_LORA, B_HEADS, B_NOPE + B_V)
    wuk = ukv[:, :, :B_NOPE].reshape(B_KV_LORA, -1).astype(BF16)
    wuvt = ukv[:, :, B_NOPE:].reshape(B_KV_LORA, -1).T.astype(BF16)
    gq = jnp.concatenate([_tile_gain(q_nope_g, B_HEADS), _tile_gain(q_rope_g, B_HEADS * B_NOPE // B_ROPE)], axis=1)
    gk = jnp.concatenate([_tile_gain(k_nope_g, B_HEADS), _tile_gain(k_rope_g, LANES // B_ROPE)], axis=1)
    cos, sin = _rope_tables_std(s, B_ROPE)
    q, k, vt = _mla_proj(x, g, wd, q_lat_g.astype(F32)[None, :], kv_lat_g.astype(F32)[None, :], wuq, wuk, wuvt,
                         _block_diag(B_NOPE), _block_diag(B_ROPE), gq, gk, cos, sin,
                         scale=(B_NOPE + B_ROPE) ** -0.5 * LOG2E, seq=s, tm=_pick(s, 512))
    return _attention(q.reshape(b, s, -1), k.reshape(b, s, -1), vt, mode="pair", qw=2 * LANES, dv=B_V, nv=2,
                      kmap=lambda p: p, vmap=lambda p: p, tq=_pick(s, ATT_TQ))


def _mixer_c(x, b, s, g, w_qkv, q_g, k_g, lq1, lk1, lq2, lk2, subln_g, layer_idx):
    nq = 2 * C_HEADS * HEAD_DIM
    gain = jnp.concatenate([_tile_gain(q_g, nq // HEAD_DIM), _tile_gain(k_g, nq // HEAD_DIM)], axis=1)
    cos, sin = _rope_tables_std(s, HEAD_DIM)
    q, k, vt = _qkv_proj(x, g, w_qkv[:, :2 * nq].astype(BF16), w_qkv[:, 2 * nq:].T.astype(BF16), gain, cos, sin,
                         _block_diag(HEAD_DIM), nq=nq, nk=nq, half=HEAD_DIM // 2, q_scale=HEAD_DIM ** -0.5 * LOG2E,
                         v_heads=C_HEADS, dv=2 * HEAD_DIM, seq=s, tm=_pick(s, 512))
    lam_init = LAMBDA_INIT_BASE - LAMBDA_INIT_AMP * math.exp(-LAMBDA_INIT_RATE * layer_idx)
    lam = (jnp.exp(jnp.sum(lq1.astype(F32) * lk1.astype(F32))) - jnp.exp(jnp.sum(lq2.astype(F32) * lk2.astype(F32)))
           + lam_init)
    lam_arr = jnp.stack([lam, jnp.asarray(lam_init, F32)]).reshape(1, 2).astype(F32)
    return _attention(q.reshape(b, s, -1), k.reshape(b, s, -1), vt, mode="diff", qw=LANES, dv=2 * HEAD_DIM, nv=1,
                      kmap=lambda p: p, vmap=lambda p: p, tq=_pick(s, ATT_TQ), lam=lam_arr,
                      subg=subln_g.astype(F32)[None, :])


def _mixer_d(x, b, s, g, w_qkv, q_g, k_g):
    nq = D_HEADS * HEAD_DIM
    gain = jnp.concatenate([_tile_gain(q_g, D_HEADS), _tile_gain(k_g, D_HEADS)], axis=1)
    cos, sin = _rope_tables_std(s, HEAD_DIM)
    q, k, vt = _qkv_proj(x, g, w_qkv[:, :2 * nq].astype(BF16), w_qkv[:, 2 * nq:].T.astype(BF16), gain, cos, sin,
                         _block_diag(HEAD_DIM), nq=nq, nk=nq, half=HEAD_DIM // 2, q_scale=HEAD_DIM ** -0.5 * LOG2E,
                         v_heads=D_HEADS, dv=HEAD_DIM, seq=s, tm=_pick(s, 512))
    reach = max(w // 2 for w, _ in DILATED_BRANCHES)
    band = -(-reach // ATT_BLK)
    tq = _pick(s, ATT_TQ if ATT_TQ // ATT_BLK + 2 * band >= s // ATT_BLK else 2 * ATT_BLK)
    return _attention(q.reshape(b, s, -1), k.reshape(b, s, -1), vt, mode="pair", qw=LANES, dv=HEAD_DIM, nv=2,
                      kmap=lambda p: p, vmap=lambda p: p, tq=tq, bias=_dilated_bias(band), band=band)


def _ec_ffn(xnew, xn, aff, wg, wu, wd, layer):
    n, d = xnew.shape
    cap = (EC_CAPACITY * n) // N_EXPERTS
    gate, idx = lax.top_k(aff, cap)
    y = _expert_ffn(xn[idx], wg, wu, wd, gate[..., None], layer=layer, tm=_pick(cap, 512))
    flat = idx.reshape(-1)
    tok_sorted, order = lax.sort_key_val(flat, jnp.arange(flat.size, dtype=jnp.int32))
    bounds = jnp.arange(0, n + 1, SEG_TOK, dtype=jnp.int32)
    offs = jnp.sum((tok_sorted[None, :] < bounds[:, None]).astype(jnp.int32), axis=1)
    return _combine(offs, xnew, tok_sorted.reshape(-1, 1, SEG_ROWS), y.reshape(-1, d)[order])


def kernel(x_prompt, x_sample, norm_mix_g, norm_ffn_g, a_w_qkv, a_q_norm_g, a_k_norm_g, a_w_o, b_w_down, b_q_lat_norm_g, b_kv_lat_norm_g, b_w_uq, b_w_ukv, b_q_nope_norm_g, b_q_rope_norm_g, b_k_nope_norm_g, b_k_rope_norm_g, b_w_o, c_w_qkv, c_q_norm_g, c_k_norm_g, c_lambda_q1, c_lambda_k1, c_lambda_q2, c_lambda_k2, c_subln_g, c_w_o, d_w_qkv, d_q_norm_g, d_k_norm_g, d_w_o, ec_w_router, ec_w_gate, ec_w_up, ec_w_down):
    depth = norm_mix_g.shape[0]
    wg_all, wu_all, wd_all = ec_w_gate.astype(BF16), ec_w_up.astype(BF16), ec_w_down.astype(BF16)

    def trunk(x3):
        b, s, d = x3.shape
        x = x3.reshape(b * s, d)
        for i in range(depth):
            m, j = i % N_MIXERS, i // N_MIXERS
            g = norm_mix_g[i].astype(F32)[None, :]
            if m == 0:
                o, w_o = _mixer_a(x, b, s, g, a_w_qkv[j], a_q_norm_g[j], a_k_norm_g[j]), a_w_o[j]
            elif m == 1:
                o = _mixer_b(x, b, s, g, b_w_down[j], b_q_lat_norm_g[j], b_kv_lat_norm_g[j], b_w_uq[j], b_w_ukv[j],
                             b_q_nope_norm_g[j], b_q_rope_norm_g[j], b_k_nope_norm_g[j], b_k_rope_norm_g[j])
                w_o = b_w_o[j]
            elif m == 2:
                o = _mixer_c(x, b, s, g, c_w_qkv[j], c_q_norm_g[j], c_k_norm_g[j], c_lambda_q1[j], c_lambda_k1[j],
                             c_lambda_q2[j], c_lambda_k2[j], c_subln_g[j], i)
                w_o = c_w_o[j]
            else:
                o, w_o = _mixer_d(x, b, s, g, d_w_qkv[j], d_q_norm_g[j], d_k_norm_g[j]), d_w_o[j]
            xnew, xn, aff = _oproj(o.reshape(b * s, -1), w_o.astype(BF16), x, norm_ffn_g[i].astype(F32)[None, :],
                                   ec_w_router[i].T.astype(BF16), tm=_pick(b * s, 512))
            x = _ec_ffn(xnew, xn, aff, wg_all, wu_all, wd_all, i)
        return x.reshape(b, s, d)

    return (trunk(x_prompt), trunk(x_sample))
```

```python
import functools
import math

import jax
import jax.numpy as jnp
from jax import lax
from jax.experimental import pallas as pl
from jax.experimental.pallas import tpu as pltpu

F32 = jnp.float32
BF16 = jnp.bfloat16

D_MODEL = 1024
HEAD_DIM = 64
GRID_W = 64
ROPE_THETA = 10000.0
RMS_EPS = 1e-6
N_MIXERS = 4
A_HEADS, A_KV_HEADS = 16, 4
B_HEADS, B_Q_LORA, B_KV_LORA, B_NOPE, B_ROPE, B_V = 16, 384, 256, 64, 32, 64
C_HEADS = 8
LAMBDA_INIT_BASE, LAMBDA_INIT_AMP, LAMBDA_INIT_RATE = 0.8, 0.6, 0.3
D_HEADS = 16
DILATED_BRANCHES = ((128, 1), (512, 4), (2048, 16))
N_EXPERTS = 16
EC_CAPACITY = 2
EXPERT_FF = 2 * D_MODEL

LANES = 128
MXU_DIM = 256
VMEM_LIMIT = 56 * 1024 * 1024
V_ONES = 16
ATT_BLK = MXU_DIM
ATT_CHAINS = 32
ATT_TQ = (ATT_CHAINS // 2) * ATT_BLK
SEG_TOK = MXU_DIM
SEG_ROWS_LOG2 = 8
SEG_ROWS = 1 << SEG_ROWS_LOG2
SEG_STEP_TOK = 8 * SEG_TOK
SEG_BUFS = 4
NEG_BIG = -1e30
LOG2E = math.log2(math.e)


def _cparams(sem):
    return pltpu.CompilerParams(dimension_semantics=sem, vmem_limit_bytes=VMEM_LIMIT)


_NT = (((1,), (1,)), ((), ()))


def _rms_rows(x, g):
    return x * lax.rsqrt(jnp.mean(x * x, axis=-1, keepdims=True) + RMS_EPS) * g


def _head_rms(t, bd, gain, hd):
    outs = []
    for j in range(t.shape[1] // MXU_DIM):
        tj = t[:, j * MXU_DIM:(j + 1) * MXU_DIM]
        ss = jnp.dot((tj * tj).astype(BF16), bd, preferred_element_type=F32)
        outs.append(tj * lax.rsqrt(ss * (1.0 / hd) + RMS_EPS))
    y = outs[0] if len(outs) == 1 else jnp.concatenate(outs, axis=-1)
    return y * gain


def _rope_slabs(y, cos, sin, half):
    lane = lax.broadcasted_iota(jnp.int32, (1, LANES), 1)
    first = (lane % (2 * half)) < half
    outs = []
    for j in range(y.shape[1] // LANES):
        yj = y[:, j * LANES:(j + 1) * LANES]
        swapped = jnp.where(first, pltpu.roll(yj, LANES - half, 1), pltpu.roll(yj, half, 1))
        outs.append(yj * cos + swapped * sin)
    return outs[0] if len(outs) == 1 else jnp.concatenate(outs, axis=-1)


def _store_vt(vt_ref, vt, *, heads, dv):
    hv = dv + V_ONES
    ones = jnp.ones((V_ONES, ATT_BLK), BF16)
    vt = vt.astype(BF16)
    for j in range(vt.shape[1] // ATT_BLK):
        for h in range(heads):
            vt_ref[0, j, h * hv:h * hv + dv, :] = vt[h * dv:(h + 1) * dv, j * ATT_BLK:(j + 1) * ATT_BLK]
            vt_ref[0, j, h * hv + dv:(h + 1) * hv, :] = ones


def _qkv_kernel(x_ref, g_ref, w_ref, wvt_ref, bd_ref, gain_ref, cos_ref, sin_ref, q_ref, k_ref, vt_ref,
                *, nq, half, q_scale, v_heads, dv):
    xn = _rms_rows(x_ref[...], g_ref[...]).astype(BF16)
    y = jnp.dot(xn, w_ref[...], preferred_element_type=F32)
    bd = bd_ref[...]
    cos, sin = cos_ref[...], sin_ref[...]
    gain = gain_ref[...]
    q = _rope_slabs(_head_rms(y[:, :nq], bd, gain[:, :nq], HEAD_DIM), cos, sin, half) * q_scale
    k = _rope_slabs(_head_rms(y[:, nq:], bd, gain[:, nq:], HEAD_DIM), cos, sin, half)
    q_ref[...] = q.astype(BF16)
    k_ref[...] = k.astype(BF16)
    vt = lax.dot_general(wvt_ref[...], xn, _NT, preferred_element_type=F32)
    _store_vt(vt_ref, vt, heads=v_heads, dv=dv)


def _vt_out(n, seq, tm, vr):
    nt = seq // tm
    spec = pl.BlockSpec((1, tm // ATT_BLK, vr, ATT_BLK), lambda i: (i // nt, i % nt, 0, 0))
    return spec, jax.ShapeDtypeStruct((n // seq, seq // ATT_BLK, vr, ATT_BLK), BF16)


def _qkv_proj(x, g, w, wvt, gain, cos, sin, bd, *, nq, nk, half, q_scale, v_heads, dv, seq, tm):
    n, d = x.shape
    nt = seq // tm
    row = lambda i: (i, 0)
    const = lambda i: (0, 0)
    vt_spec, vt_shape = _vt_out(n, seq, tm, v_heads * (dv + V_ONES))
    return pl.pallas_call(
        functools.partial(_qkv_kernel, nq=nq, half=half, q_scale=q_scale, v_heads=v_heads, dv=dv),
        grid=(n // tm,),
        in_specs=[pl.BlockSpec((tm, d), row), pl.BlockSpec((1, d), const),
                  pl.BlockSpec(w.shape, const), pl.BlockSpec(wvt.shape, const), pl.BlockSpec(bd.shape, const),
                  pl.BlockSpec(gain.shape, const),
                  pl.BlockSpec((tm, LANES), lambda i: (i % nt, 0)),
                  pl.BlockSpec((tm, LANES), lambda i: (i % nt, 0))],
        out_specs=[pl.BlockSpec((tm, nq), row), pl.BlockSpec((tm, nk), row), vt_spec],
        out_shape=[jax.ShapeDtypeStruct((n, nq), BF16), jax.ShapeDtypeStruct((n, nk), BF16), vt_shape],
        compiler_params=_cparams(("parallel",)),
        name="qkv_proj",
    )(x, g, w, wvt, bd, gain, cos, sin)


def _mla_kernel(x_ref, g_ref, wd_ref, qlg_ref, kvlg_ref, wuq_ref, wuk_ref, wuvt_ref, bd64_ref, bd32_ref,
                gq_ref, gk_ref, cos_ref, sin_ref, q_ref, k_ref, vt_ref, *, scale):
    xn = _rms_rows(x_ref[...], g_ref[...]).astype(BF16)
    c = jnp.dot(xn, wd_ref[...], preferred_element_type=F32)
    cq = _rms_rows(c[:, :B_Q_LORA], qlg_ref[...]).astype(BF16)
    ckv = _rms_rows(c[:, B_Q_LORA:B_Q_LORA + B_KV_LORA], kvlg_ref[...]).astype(BF16)
    q = jnp.dot(cq, wuq_ref[...], preferred_element_type=F32)
    kn = jnp.dot(ckv, wuk_ref[...], preferred_element_type=F32)
    bd64, bd32 = bd64_ref[...], bd32_ref[...]
    cos, sin = cos_ref[...], sin_ref[...]
    gq, gk = gq_ref[...], gk_ref[...]
    hw = B_HEADS * B_NOPE
    qn = _head_rms(q[:, :hw], bd64, gq[:, :hw], B_NOPE) * scale
    qr = _rope_slabs(_head_rms(q[:, hw:], bd32, gq[:, hw:], B_ROPE), cos, sin, B_ROPE // 2) * scale
    kn = _head_rms(kn, bd64, gk[:, :hw], B_NOPE)
    kr_raw = c[:, B_Q_LORA + B_KV_LORA:]
    kr2 = _head_rms(jnp.concatenate([kr_raw, kr_raw], axis=-1), bd32,
                    jnp.concatenate([gk[:, hw:], gk[:, hw:]], axis=-1), B_ROPE)
    kr = _rope_slabs(kr2[:, :LANES], cos, sin, B_ROPE // 2).astype(BF16)
    qn, qr, kn = qn.astype(BF16), qr.astype(BF16), kn.astype(BF16)
    for p in range(B_HEADS // 2):
        sl = slice(p * LANES, (p + 1) * LANES)
        q_ref[:, 2 * p * LANES:(2 * p + 1) * LANES] = qn[:, sl]
        q_ref[:, (2 * p + 1) * LANES:(2 * p + 2) * LANES] = qr[:, sl]
        k_ref[:, 2 * p * LANES:(2 * p + 1) * LANES] = kn[:, sl]
        k_ref[:, (2 * p + 1) * LANES:(2 * p + 2) * LANES] = kr
    vt = lax.dot_general(wuvt_ref[...], ckv, _NT, preferred_element_type=F32)
    _store_vt(vt_ref, vt, heads=B_HEADS, dv=B_V)


def _mla_proj(x, g, wd, qlg, kvlg, wuq, wuk, wuvt, bd64, bd32, gq, gk, cos, sin, *, scale, seq, tm):
    n, d = x.shape
    nt = seq // tm
    row = lambda i: (i, 0)
    const = lambda i: (0, 0)
    full = lambda a: pl.BlockSpec(a.shape, const)
    wq = (B_HEADS // 2) * 2 * LANES
    vt_spec, vt_shape = _vt_out(n, seq, tm, B_HEADS * (B_V + V_ONES))
    return pl.pallas_call(
        functools.partial(_mla_kernel, scale=scale),
        grid=(n // tm,),
        in_specs=[pl.BlockSpec((tm, d), row), full(g), full(wd), full(qlg), full(kvlg), full(wuq), full(wuk),
                  full(wuvt), full(bd64), full(bd32), full(gq), full(gk),
                  pl.BlockSpec((tm, LANES), lambda i: (i % nt, 0)),
                  pl.BlockSpec((tm, LANES), lambda i: (i % nt, 0))],
        out_specs=[pl.BlockSpec((tm, wq), row), pl.BlockSpec((tm, wq), row), vt_spec],
        out_shape=[jax.ShapeDtypeStruct((n, wq), BF16), jax.ShapeDtypeStruct((n, wq), BF16), vt_shape],
        compiler_params=_cparams(("parallel",)),
        name="mla_proj",
    )(x, g, wd, qlg, kvlg, wuq, wuk, wuvt, bd64, bd32, gq, gk, cos, sin)


def _attn_kernel(*refs, mode, qw, dv, nv, tq, nk, band, pp, share):
    it = iter(refs)
    q_ref, k_ref, vt_ref = next(it), next(it), next(it)
    bias_ref = next(it) if band is not None else None
    if mode == "diff":
        lam_ref, subg_ref = next(it), next(it)
    o_ref = next(it)
    qm_sc, m_sc, acc_sc, sa_sc, sb_sc = next(it), next(it), next(it), next(it), next(it)

    blk = ATT_BLK
    hv = dv + V_ONES
    nsub = tq // blk
    chains = [(pi, r, h) for pi in range(pp) for r in range(nsub) for h in range(2)]
    qi = pl.program_id(2)

    lane = lax.broadcasted_iota(jnp.int32, (1, qw), 1)
    if qw == LANES:
        masks = (lane < HEAD_DIM, lane >= HEAD_DIM)
    else:
        masks = ((lane < HEAD_DIM) | ((lane >= LANES) & (lane < LANES + B_ROPE)),
                 ((lane >= HEAD_DIM) & (lane < LANES)) | ((lane >= LANES + B_ROPE) & (lane < LANES + 2 * B_ROPE)))
    for ci, (pi, r, h) in enumerate(chains):
        qr = q_ref[0, r * blk:(r + 1) * blk, pi * qw:(pi + 1) * qw]
        qm_sc[ci] = jnp.where(masks[h], qr, jnp.zeros_like(qr))
    m_sc[...] = jnp.full(m_sc.shape, NEG_BIG, F32)
    acc_sc[...] = jnp.zeros(acc_sc.shape, F32)

    def score_chain(ci, c, k, s_sc):
        pi = 0 if share else chains[ci][0]
        st = lax.dot_general(k[:, pi * qw:(pi + 1) * qw], qm_sc[ci], _NT,
                             preferred_element_type=F32)
        if band is not None:
            rel = c - (qi * nsub + chains[ci][1]) + band + 1
            st = st + bias_ref[jnp.clip(rel, 0, 2 * band + 2)]
        s_sc[ci] = st

    def consume_chain(ci, vt, s_sc):
        st = s_sc[ci]
        m_old = m_sc[ci]
        m_new = jnp.maximum(m_old, jnp.max(st, axis=0, keepdims=True))
        p = jnp.exp2(st - m_new).astype(BF16)
        alpha = jnp.exp2(m_old - m_new)
        m_sc[ci] = m_new
        pi, _, h = chains[ci]
        voff = ((0 if share else pi * nv) + (h if nv == 2 else 0)) * hv
        pv = jnp.dot(vt[voff:voff + hv], p, preferred_element_type=F32)
        acc_sc[ci] = acc_sc[ci] * alpha + pv

    def step(c_cur, s_cur, c_next=None, s_next=None):
        vt = vt_ref[0, c_cur]
        if c_next is not None:
            k = k_ref[0, pl.ds(pl.multiple_of(c_next * blk, blk), blk), :]
        for ci in range(len(chains)):
            if c_next is not None:
                score_chain(ci, c_next, k, s_next)
            consume_chain(ci, vt, s_cur)

    if band is None:
        lo, hi = 0, nk
    else:
        lo, hi = jnp.maximum(qi * nsub - band, 0), jnp.minimum((qi + 1) * nsub + band, nk)
    npairs = (hi - lo - 1) // 2

    def body(i, carry):
        c = lo + 2 * i
        step(c, sa_sc, c + 1, sb_sc)
        step(c + 1, sb_sc, c + 2, sa_sc)
        return carry

    k0 = k_ref[0, pl.ds(pl.multiple_of(lo * blk, blk), blk), :]
    for ci in range(len(chains)):
        score_chain(ci, lo, k0, sa_sc)
    lax.fori_loop(0, npairs, body, 0)
    c_tail = lo + 2 * npairs
    if band is None:
        if (hi - lo) % 2 == 0:
            step(c_tail, sa_sc, c_tail + 1, sb_sc)
            step(c_tail + 1, sb_sc)
        else:
            step(c_tail, sa_sc)
    else:
        two_left = (hi - c_tail) == 2

        @pl.when(two_left)
        def _():
            step(c_tail, sa_sc, c_tail + 1, sb_sc)
            step(c_tail + 1, sb_sc)

        @pl.when(jnp.logical_not(two_left))
        def _():
            step(c_tail, sa_sc)

    for pi, r in [(pi, r) for pi in range(pp) for r in range(nsub)]:
        a, b = acc_sc[2 * (pi * nsub + r)], acc_sc[2 * (pi * nsub + r) + 1]
        oa = a[:dv] / a[dv:dv + 1]
        ob = b[:dv] / b[dv:dv + 1]
        if mode == "pair":
            o = jnp.concatenate([oa, ob], axis=0).T
        else:
            o = (oa - lam_ref[0, 0] * ob).T
            o = _rms_rows(o, subg_ref[...]) * (1.0 - lam_ref[0, 1])
        o_ref[0, r * blk:(r + 1) * blk, pi * LANES:(pi + 1) * LANES] = o.astype(o_ref.dtype)


def _attention(q, k, vt, *, mode, qw, dv, nv, share, tq, bias=None, band=None, lam=None, subg=None):
    b, s, cq = q.shape
    pairs = cq // qw
    nk = s // ATT_BLK
    hv = dv + V_ONES
    pp = share if share else max(1, min(2, ATT_CHAINS // (2 * (tq // ATT_BLK))))
    kpp = 1 if share else pp
    in_specs = [pl.BlockSpec((1, tq, pp * qw), lambda bi, p, i: (bi, i, p)),
                pl.BlockSpec((1, s, kpp * qw), lambda bi, p, i: (bi, 0, p)),
                pl.BlockSpec((1, nk, kpp * nv * hv, ATT_BLK), lambda bi, p, i: (bi, 0, p, 0))]
    args = [q, k, vt]
    if band is not None:
        in_specs.append(pl.BlockSpec(bias.shape, lambda bi, p, i: (0, 0, 0)))
        args.append(bias)
    if mode == "diff":
        in_specs.append(pl.BlockSpec(memory_space=pltpu.SMEM))
        in_specs.append(pl.BlockSpec(subg.shape, lambda bi, p, i: (0, 0)))
        args += [lam, subg]
    nchains = 2 * pp * (tq // ATT_BLK)
    return pl.pallas_call(
        functools.partial(_attn_kernel, mode=mode, qw=qw, dv=dv, nv=nv, tq=tq, nk=nk, band=band, pp=pp,
                          share=bool(share)),
        grid=(b, pairs // pp, s // tq),
        in_specs=in_specs,
        out_specs=pl.BlockSpec((1, tq, pp * LANES), lambda bi, p, i: (bi, i, p)),
        out_shape=jax.ShapeDtypeStruct((b, s, pairs * LANES), BF16),
        scratch_shapes=[pltpu.VMEM((nchains, ATT_BLK, qw), BF16), pltpu.VMEM((nchains, 1, ATT_BLK), F32),
                        pltpu.VMEM((nchains, hv, ATT_BLK), F32),
                        pltpu.VMEM((nchains, ATT_BLK, ATT_BLK), F32), pltpu.VMEM((nchains, ATT_BLK, ATT_BLK), F32)],
        compiler_params=_cparams(("parallel", "parallel", "arbitrary")),
        name="attention_" + mode,
    )(*args)


def _oproj_kernel(o_ref, w_ref, x_ref, g_ref, wr_ref, xnew_ref, xn_ref, aff_ref):
    xnew = x_ref[...] + jnp.dot(o_ref[...], w_ref[...], preferred_element_type=F32)
    xnew_ref[...] = xnew
    xn = _rms_rows(xnew, g_ref[...])
    xn_ref[...] = xn
    logits = lax.dot_general(wr_ref[...], xn.astype(BF16), _NT, preferred_element_type=F32)
    z = logits - jnp.max(logits, axis=0, keepdims=True)
    e = jnp.exp(z)
    aff_ref[...] = e / jnp.sum(e, axis=0, keepdims=True)


def _oproj(o, w, x, g, wr_t, *, tm):
    n, d = x.shape
    row = lambda i: (i, 0)
    const = lambda i: (0, 0)
    return pl.pallas_call(
        _oproj_kernel,
        grid=(n // tm,),
        in_specs=[pl.BlockSpec((tm, o.shape[1]), row), pl.BlockSpec(w.shape, const), pl.BlockSpec((tm, d), row),
                  pl.BlockSpec((1, d), const), pl.BlockSpec(wr_t.shape, const)],
        out_specs=[pl.BlockSpec((tm, d), row), pl.BlockSpec((tm, d), row),
                   pl.BlockSpec((N_EXPERTS, tm), lambda i: (0, i))],
        out_shape=[jax.ShapeDtypeStruct((n, d), F32), jax.ShapeDtypeStruct((n, d), F32),
                   jax.ShapeDtypeStruct((N_EXPERTS, n), F32)],
        compiler_params=_cparams(("parallel",)),
        name="oproj_router",
    )(o, w, x, g, wr_t)


def _ffn_kernel(xg_ref, wg_ref, wu_ref, wd_ref, gate_ref, y_ref):
    xg = xg_ref[0].astype(BF16)
    a = jnp.dot(xg, wg_ref[0, 0], preferred_element_type=F32)
    u = jnp.dot(xg, wu_ref[0, 0], preferred_element_type=F32)
    h = (a * jax.nn.sigmoid(a) * u).astype(BF16)
    y_ref[0] = jnp.dot(h, wd_ref[0, 0], preferred_element_type=F32) * gate_ref[0]


def _expert_ffn(xg, wg, wu, wd, gate, *, layer, tm):
    e, cap, d = xg.shape
    ff = wg.shape[3]
    return pl.pallas_call(
        _ffn_kernel,
        grid=(e, cap // tm),
        in_specs=[pl.BlockSpec((1, tm, d), lambda ei, i: (ei, i, 0)),
                  pl.BlockSpec((1, 1, d, ff), lambda ei, i: (layer, ei, 0, 0)),
                  pl.BlockSpec((1, 1, d, ff), lambda ei, i: (layer, ei, 0, 0)),
                  pl.BlockSpec((1, 1, ff, d), lambda ei, i: (layer, ei, 0, 0)),
                  pl.BlockSpec((1, tm, 1), lambda ei, i: (ei, i, 0))],
        out_specs=pl.BlockSpec((1, tm, d), lambda ei, i: (ei, i, 0)),
        out_shape=jax.ShapeDtypeStruct((e, cap, d), F32),
        compiler_params=_cparams(("parallel", "arbitrary")),
        name="expert_ffn",
    )(xg, wg, wu, wd, gate)


def _combine_kernel(offs_ref, x_ref, tok_hbm, z_hbm, o_ref, tokbuf, zbuf, sem, *, nsub):
    step = pl.program_id(0)
    blk0 = step * nsub
    c_lo = lax.shift_right_logical(offs_ref[blk0], SEG_ROWS_LOG2)
    c_hi = lax.shift_right_logical(offs_ref[blk0 + nsub] + (SEG_ROWS - 1), SEG_ROWS_LOG2)
    o_ref[...] = x_ref[...]

    def copies(c, slot):
        return (pltpu.make_async_copy(tok_hbm.at[c], tokbuf.at[slot], sem.at[0, slot]),
                pltpu.make_async_copy(z_hbm.at[pl.ds(pl.multiple_of(c * SEG_ROWS, SEG_ROWS), SEG_ROWS)],
                                      zbuf.at[slot], sem.at[1, slot]))

    for j in range(SEG_BUFS - 1):
        @pl.when(c_lo + j < c_hi)
        def _():
            for cp in copies(c_lo + j, j):
                cp.start()

    def body(c, carry):
        slot = lax.rem(c - c_lo, SEG_BUFS)
        for cp in copies(c, slot):
            cp.wait()
        ahead = c + (SEG_BUFS - 1)

        @pl.when(ahead < c_hi)
        def _():
            for cp in copies(ahead, lax.rem(ahead - c_lo, SEG_BUFS)):
                cp.start()

        tok = tokbuf[slot]
        z = zbuf[slot]
        zh = z.astype(BF16)
        zl = (z - zh.astype(F32)).astype(BF16)
        row0 = c * SEG_ROWS
        u_first, u_end = jnp.int32(0), jnp.int32(0)
        for u in range(nsub):
            u_first = u_first + (offs_ref[blk0 + u + 1] <= row0).astype(jnp.int32)
            u_end = u_end + (offs_ref[blk0 + u] < row0 + SEG_ROWS).astype(jnp.int32)

        def add_block(u, valid):
            base = jnp.where(valid, (blk0 + u) * SEG_TOK, -2 * SEG_TOK)
            tid = base + lax.broadcasted_iota(jnp.int32, (SEG_TOK, SEG_ROWS), 0)
            onehot = jnp.where(tok == tid, 1.0, 0.0).astype(BF16)
            upd = (jnp.dot(onehot, zh, preferred_element_type=F32)
                   + jnp.dot(onehot, zl, preferred_element_type=F32))
            start = pl.multiple_of(jnp.minimum(u, nsub - 1) * SEG_TOK, SEG_TOK)
            o_ref[pl.ds(start, SEG_TOK), :] += upd

        add_block(u_first, u_first < u_end)
        add_block(u_first + 1, u_first + 1 < u_end)

        def more(u, cr):
            add_block(u, True)
            return cr

        lax.fori_loop(u_first + 2, u_end, more, 0)
        return carry

    lax.fori_loop(c_lo, c_hi, body, 0)


def _combine(offs, x, tok, z):
    n, d = x.shape
    tn = _pick(n, SEG_STEP_TOK)
    return pl.pallas_call(
        functools.partial(_combine_kernel, nsub=tn // SEG_TOK),
        grid_spec=pltpu.PrefetchScalarGridSpec(
            num_scalar_prefetch=1,
            grid=(n // tn,),
            in_specs=[pl.BlockSpec((tn, d), lambda i, offs: (i, 0)),
                      pl.BlockSpec(memory_space=pl.ANY), pl.BlockSpec(memory_space=pl.ANY)],
            out_specs=pl.BlockSpec((tn, d), lambda i, offs: (i, 0)),
            scratch_shapes=[pltpu.VMEM((SEG_BUFS, 1, SEG_ROWS), jnp.int32),
                            pltpu.VMEM((SEG_BUFS, SEG_ROWS, d), F32),
                            pltpu.SemaphoreType.DMA((2, SEG_BUFS))]),
        out_shape=jax.ShapeDtypeStruct((n, d), F32),
        compiler_params=_cparams(("arbitrary",)),
        name="combine_segsum",
    )(offs, x, tok, z)


def _block_diag(hd):
    i = jnp.arange(MXU_DIM)
    return (i[:, None] // hd == i[None, :] // hd).astype(BF16)


def _rope_tables_std(seq, dim):
    half = dim // 2
    inv_freq = ROPE_THETA ** (-jnp.arange(half, dtype=F32) * 2.0 / dim)
    ang = jnp.arange(seq).astype(F32)[:, None] * inv_freq[None, :]
    cos, sin = jnp.cos(ang), jnp.sin(ang)
    reps = LANES // dim
    return (jnp.tile(jnp.concatenate([cos, cos], axis=-1), (1, reps)),
            jnp.tile(jnp.concatenate([-sin, sin], axis=-1), (1, reps)))


def _rope_tables_axial(seq):
    sub = HEAD_DIM // 2
    half = sub // 2
    inv_freq = ROPE_THETA ** (-jnp.arange(half, dtype=F32) * 2.0 / sub)
    n_rows = seq // GRID_W
    rows = jnp.repeat(jnp.arange(n_rows), GRID_W).astype(F32)
    cols = jnp.tile(jnp.arange(GRID_W), n_rows).astype(F32)
    ar, ac = rows[:, None] * inv_freq[None, :], cols[:, None] * inv_freq[None, :]
    cos = jnp.concatenate([jnp.cos(ar), jnp.cos(ar), jnp.cos(ac), jnp.cos(ac)], axis=-1)
    sin = jnp.concatenate([-jnp.sin(ar), jnp.sin(ar), -jnp.sin(ac), jnp.sin(ac)], axis=-1)
    return jnp.tile(cos, (1, 2)), jnp.tile(sin, (1, 2))


def _dilated_bias(band):
    t = ATT_BLK
    rel = jnp.arange(-band - 1, band + 2)[:, None, None] * t
    d = rel + jnp.arange(t)[None, :, None] - jnp.arange(t)[None, None, :]
    cnt = jnp.zeros(d.shape, F32)
    for window, dil in DILATED_BRANCHES:
        cnt = cnt + ((d % dil == 0) & (jnp.abs(d) <= window // 2)).astype(F32)
    return jnp.where(cnt > 0, jnp.log2(jnp.maximum(cnt, 1.0)), NEG_BIG)


def _tile_gain(g, reps):
    return jnp.tile(g.astype(F32), reps)[None, :]


def _pick(n, pref):
    t = min(n, pref)
    while n % t:
        t //= 2
    return t


def _mixer_a(x, b, s, g, w_qkv, q_g, k_g):
    nq = A_HEADS * HEAD_DIM
    nkv = A_KV_HEADS * HEAD_DIM
    wq, wk, wv = w_qkv[:, :nq], w_qkv[:, nq:nq + nkv], w_qkv[:, nq + nkv:]
    wk2 = jnp.repeat(wk.reshape(D_MODEL, A_KV_HEADS, 1, HEAD_DIM), 2, axis=2).reshape(D_MODEL, -1)
    w = jnp.concatenate([wq, wk2], axis=1).astype(BF16)
    nk = 2 * nkv
    gain = jnp.concatenate([_tile_gain(q_g, nq // HEAD_DIM), _tile_gain(k_g, nk // HEAD_DIM)], axis=1)
    cos, sin = _rope_tables_axial(s)
    q, k, vt = _qkv_proj(x, g, w, wv.T.astype(BF16), gain, cos, sin, _block_diag(HEAD_DIM), nq=nq, nk=nk,
                         half=HEAD_DIM // 4, q_scale=HEAD_DIM ** -0.5 * LOG2E, v_heads=A_KV_HEADS, dv=HEAD_DIM,
                         seq=s, tm=_pick(s, 512))
    group_pairs = (A_HEADS // A_KV_HEADS) // 2
    return _attention(q.reshape(b, s, -1), k.reshape(b, s, -1), vt, mode="pair", qw=LANES, dv=HEAD_DIM, nv=1,
                      share=group_pairs, tq=_pick(s, ATT_TQ // group_pairs))


def _mixer_b(x, b, s, g, w_down, q_lat_g, kv_lat_g, w_uq, w_ukv, q_nope_g, q_rope_g, k_nope_g, k_rope_g):
    zpad = lambda a, n: jnp.concatenate([a, jnp.zeros(a.shape[:-1] + (n,), a.dtype)], axis=-1)
    w_kr = w_down[:, B_Q_LORA + B_KV_LORA:]
    wd = jnp.concatenate([w_down[:, :B_Q_LORA + B_KV_LORA], zpad(jnp.concatenate([w_kr, w_kr], axis=1), 2 * B_ROPE)],
                         axis=1).astype(BF16)
    uq = w_uq.reshape(B_Q_LORA, B_HEADS, B_NOPE + B_ROPE)
    uq_n = uq[:, :, :B_NOPE].reshape(B_Q_LORA, -1)
    uq_r = zpad(uq[:, :, B_NOPE:].reshape(B_Q_LORA, B_HEADS // 2, 2 * B_ROPE), LANES - 2 * B_ROPE).reshape(B_Q_LORA, -1)
    wuq = jnp.concatenate([uq_n, uq_r], axis=1).astype(BF16)
    ukv = w_ukv.reshape(B_KV_LORA, B_HEADS, B_NOPE + B_V)
    wuk = ukv[:, :, :B_NOPE].reshape(B_KV_LORA, -1).astype(BF16)
    wuvt = ukv[:, :, B_NOPE:].reshape(B_KV_LORA, -1).T.astype(BF16)
    gq = jnp.concatenate([_tile_gain(q_nope_g, B_HEADS), _tile_gain(q_rope_g, B_HEADS * B_NOPE // B_ROPE)], axis=1)
    gk = jnp.concatenate([_tile_gain(k_nope_g, B_HEADS), _tile_gain(k_rope_g, LANES // B_ROPE)], axis=1)
    cos, sin = _rope_tables_std(s, B_ROPE)
    q, k, vt = _mla_proj(x, g, wd, q_lat_g.astype(F32)[None, :], kv_lat_g.astype(F32)[None, :], wuq, wuk, wuvt,
                         _block_diag(B_NOPE), _block_diag(B_ROPE), gq, gk, cos, sin,
                         scale=(B_NOPE + B_ROPE) ** -0.5 * LOG2E, seq=s, tm=_pick(s, 512))
    return _attention(q.reshape(b, s, -1), k.reshape(b, s, -1), vt, mode="pair", qw=2 * LANES, dv=B_V, nv=2,
                      share=0, tq=_pick(s, ATT_TQ))


def _mixer_c(x, b, s, g, w_qkv, q_g, k_g, lq1, lk1, lq2, lk2, subln_g, layer_idx):
    nq = 2 * C_HEADS * HEAD_DIM
    gain = jnp.concatenate([_tile_gain(q_g, nq // HEAD_DIM), _tile_gain(k_g, nq // HEAD_DIM)], axis=1)
    cos, sin = _rope_tables_std(s, HEAD_DIM)
    q, k, vt = _qkv_proj(x, g, w_qkv[:, :2 * nq].astype(BF16), w_qkv[:, 2 * nq:].T.astype(BF16), gain, cos, sin,
                         _block_diag(HEAD_DIM), nq=nq, nk=nq, half=HEAD_DIM // 2, q_scale=HEAD_DIM ** -0.5 * LOG2E,
                         v_heads=C_HEADS, dv=2 * HEAD_DIM, seq=s, tm=_pick(s, 512))
    lam_init = LAMBDA_INIT_BASE - LAMBDA_INIT_AMP * math.exp(-LAMBDA_INIT_RATE * layer_idx)
    lam = (jnp.exp(jnp.sum(lq1.astype(F32) * lk1.astype(F32))) - jnp.exp(jnp.sum(lq2.astype(F32) * lk2.astype(F32)))
           + lam_init)
    lam_arr = jnp.stack([lam, jnp.asarray(lam_init, F32)]).reshape(1, 2).astype(F32)
    return _attention(q.reshape(b, s, -1), k.reshape(b, s, -1), vt, mode="diff", qw=LANES, dv=2 * HEAD_DIM, nv=1,
                      share=0, tq=_pick(s, ATT_TQ), lam=lam_arr,
                      subg=subln_g.astype(F32)[None, :])


def _mixer_d(x, b, s, g, w_qkv, q_g, k_g):
    nq = D_HEADS * HEAD_DIM
    gain = jnp.concatenate([_tile_gain(q_g, D_HEADS), _tile_gain(k_g, D_HEADS)], axis=1)
    cos, sin = _rope_tables_std(s, HEAD_DIM)
    q, k, vt = _qkv_proj(x, g, w_qkv[:, :2 * nq].astype(BF16), w_qkv[:, 2 * nq:].T.astype(BF16), gain, cos, sin,
                         _block_diag(HEAD_DIM), nq=nq, nk=nq, half=HEAD_DIM // 2, q_scale=HEAD_DIM ** -0.5 * LOG2E,
                         v_heads=D_HEADS, dv=HEAD_DIM, seq=s, tm=_pick(s, 512))
    reach = max(w // 2 for w, _ in DILATED_BRANCHES)
    band = -(-reach // ATT_BLK)
    tq = _pick(s, ATT_TQ if ATT_TQ // ATT_BLK + 2 * band >= s // ATT_BLK else 2 * ATT_BLK)
    return _attention(q.reshape(b, s, -1), k.reshape(b, s, -1), vt, mode="pair", qw=LANES, dv=HEAD_DIM, nv=2,
                      share=0, tq=tq, bias=_dilated_bias(band), band=band)


def _ec_ffn(xnew, xn, aff, wg, wu, wd, layer):
    n, d = xnew.shape
    cap = (EC_CAPACITY * n) // N_EXPERTS
    gate, idx = lax.top_k(aff, cap)
    y = _expert_ffn(xn[idx], wg, wu, wd, gate[..., None], layer=layer, tm=_pick(cap, 512))
    flat = idx.reshape(-1)
    tok_sorted, order = lax.sort_key_val(flat, jnp.arange(flat.size, dtype=jnp.int32))
    bounds = jnp.arange(0, n + 1, SEG_TOK, dtype=jnp.int32)
    offs = jnp.sum((tok_sorted[None, :] < bounds[:, None]).astype(jnp.int32), axis=1)
    return _combine(offs, xnew, tok_sorted.reshape(-1, 1, SEG_ROWS), y.reshape(-1, d)[order])


def kernel(x_prompt, x_sample, norm_mix_g, norm_ffn_g, a_w_qkv, a_q_norm_g, a_k_norm_g, a_w_o, b_w_down, b_q_lat_norm_g, b_kv_lat_norm_g, b_w_uq, b_w_ukv, b_q_nope_norm_g, b_q_rope_norm_g, b_k_nope_norm_g, b_k_rope_norm_g, b_w_o, c_w_qkv, c_q_norm_g, c_k_norm_g, c_lambda_q1, c_lambda_k1, c_lambda_q2, c_lambda_k2, c_subln_g, c_w_o, d_w_qkv, d_q_norm_g, d_k_norm_g, d_w_o, ec_w_router, ec_w_gate, ec_w_up, ec_w_down):
    depth = norm_mix_g.shape[0]
    wg_all, wu_all, wd_all = ec_w_gate.astype(BF16), ec_w_up.astype(BF16), ec_w_down.astype(BF16)

    def trunk(x3):
        b, s, d = x3.shape
        x = x3.reshape(b * s, d)
        for i in range(depth):
            m, j = i % N_MIXERS, i // N_MIXERS
            g = norm_mix_g[i].astype(F32)[None, :]
            if m == 0:
                o, w_o = _mixer_a(x, b, s, g, a_w_qkv[j], a_q_norm_g[j], a_k_norm_g[j]), a_w_o[j]
            elif m == 1:
                o = _mixer_b(x, b, s, g, b_w_down[j], b_q_lat_norm_g[j], b_kv_lat_norm_g[j], b_w_uq[j], b_w_ukv[j],
                             b_q_nope_norm_g[j], b_q_rope_norm_g[j], b_k_nope_norm_g[j], b_k_rope_norm_g[j])
                w_o = b_w_o[j]
            elif m == 2:
                o = _mixer_c(x, b, s, g, c_w_qkv[j], c_q_norm_g[j], c_k_norm_g[j], c_lambda_q1[j], c_lambda_k1[j],
                             c_lambda_q2[j], c_lambda_k2[j], c_subln_g[j], i)
                w_o = c_w_o[j]
            else:
                o, w_o = _mixer_d(x, b, s, g, d_w_qkv[j], d_q_norm_g[j], d_k_norm_g[j]), d_w_o[j]
            xnew, xn, aff = _oproj(o.reshape(b * s, -1), w_o.astype(BF16), x, norm_ffn_g[i].astype(F32)[None, :],
                                   ec_w_router[i].T.astype(BF16), tm=_pick(b * s, 512))
            x = _ec_ffn(xnew, xn, aff, wg_all, wu_all, wd_all, i)
        return x.reshape(b, s, d)

    return (trunk(x_prompt), trunk(x_sample))
```

```python
import functools
import math

import jax
import jax.numpy as jnp
from jax import lax
from jax.experimental import pallas as pl
from jax.experimental.pallas import tpu as pltpu

F32 = jnp.float32
BF16 = jnp.bfloat16

D_MODEL = 1024
HEAD_DIM = 64
GRID_W = 64
ROPE_THETA = 10000.0
RMS_EPS = 1e-6
N_MIXERS = 4
A_HEADS, A_KV_HEADS = 16, 4
B_HEADS, B_Q_LORA, B_KV_LORA, B_NOPE, B_ROPE, B_V = 16, 384, 256, 64, 32, 64
C_HEADS = 8
LAMBDA_INIT_BASE, LAMBDA_INIT_AMP, LAMBDA_INIT_RATE = 0.8, 0.6, 0.3
D_HEADS = 16
DILATED_BRANCHES = ((128, 1), (512, 4), (2048, 16))
N_EXPERTS = 16
EC_CAPACITY = 2

LANES = 128
MXU_DIM = 256
ROW_TILE = 2 * MXU_DIM
VMEM_LIMIT = 56 * 1024 * 1024
V_ONES = 16
ATT_BLK = MXU_DIM
ATT_CHAINS = 32
ATT_TQ = (ATT_CHAINS // 2) * ATT_BLK
SEG_TOK = MXU_DIM
SEG_ROWS_LOG2 = 8
SEG_ROWS = 1 << SEG_ROWS_LOG2
SEG_STEP_TOK = 8 * SEG_TOK
SEG_BUFS = 4
NEG_BIG = -1e30
LOG2E = math.log2(math.e)


def _cparams(sem):
    return pltpu.CompilerParams(dimension_semantics=sem, vmem_limit_bytes=VMEM_LIMIT)


_NT = (((1,), (1,)), ((), ()))


def _rms_rows(x, g):
    return x * lax.rsqrt(jnp.mean(x * x, axis=-1, keepdims=True) + RMS_EPS) * g


def _head_rms(t, bd, gain, hd):
    outs = []
    for j in range(t.shape[1] // MXU_DIM):
        tj = t[:, j * MXU_DIM:(j + 1) * MXU_DIM]
        ss = jnp.dot((tj * tj).astype(BF16), bd, preferred_element_type=F32)
        outs.append(tj * lax.rsqrt(ss * (1.0 / hd) + RMS_EPS))
    y = outs[0] if len(outs) == 1 else jnp.concatenate(outs, axis=-1)
    return y * gain


def _rope_slabs(y, cos, sin, half):
    lane = lax.broadcasted_iota(jnp.int32, (1, LANES), 1)
    first = (lane % (2 * half)) < half
    outs = []
    for j in range(y.shape[1] // LANES):
        yj = y[:, j * LANES:(j + 1) * LANES]
        swapped = jnp.where(first, pltpu.roll(yj, LANES - half, 1), pltpu.roll(yj, half, 1))
        outs.append(yj * cos + swapped * sin)
    return outs[0] if len(outs) == 1 else jnp.concatenate(outs, axis=-1)


def _store_vt(vt_ref, vt, *, heads, dv):
    hv = dv + V_ONES
    ones = jnp.ones((V_ONES, ATT_BLK), BF16)
    vt = vt.astype(BF16)
    for j in range(vt.shape[1] // ATT_BLK):
        for h in range(heads):
            vt_ref[0, j, h * hv:h * hv + dv, :] = vt[h * dv:(h + 1) * dv, j * ATT_BLK:(j + 1) * ATT_BLK]
            vt_ref[0, j, h * hv + dv:(h + 1) * hv, :] = ones


def _qkv_kernel(x_ref, g_ref, w_ref, wvt_ref, bd_ref, gain_ref, cos_ref, sin_ref, q_ref, k_ref, vt_ref,
                *, nq, half, q_scale, v_heads, dv):
    xn = _rms_rows(x_ref[...], g_ref[...]).astype(BF16)
    y = jnp.dot(xn, w_ref[...], preferred_element_type=F32)
    bd = bd_ref[...]
    cos, sin = cos_ref[...], sin_ref[...]
    gain = gain_ref[...]
    q = _rope_slabs(_head_rms(y[:, :nq], bd, gain[:, :nq], HEAD_DIM), cos, sin, half) * q_scale
    k = _rope_slabs(_head_rms(y[:, nq:], bd, gain[:, nq:], HEAD_DIM), cos, sin, half)
    q_ref[...] = q.astype(BF16)
    k_ref[...] = k.astype(BF16)
    vt = lax.dot_general(wvt_ref[...], xn, _NT, preferred_element_type=F32)
    _store_vt(vt_ref, vt, heads=v_heads, dv=dv)


def _vt_out(n, seq, tm, vr):
    nt = seq // tm
    spec = pl.BlockSpec((1, tm // ATT_BLK, vr, ATT_BLK), lambda i: (i // nt, i % nt, 0, 0))
    return spec, jax.ShapeDtypeStruct((n // seq, seq // ATT_BLK, vr, ATT_BLK), BF16)


def _qkv_proj(x, g, w, wvt, gain, cos, sin, bd, *, nq, nk, half, q_scale, v_heads, dv, seq, tm):
    n, d = x.shape
    nt = seq // tm
    row = lambda i: (i, 0)
    const = lambda i: (0, 0)
    vt_spec, vt_shape = _vt_out(n, seq, tm, v_heads * (dv + V_ONES))
    return pl.pallas_call(
        functools.partial(_qkv_kernel, nq=nq, half=half, q_scale=q_scale, v_heads=v_heads, dv=dv),
        grid=(n // tm,),
        in_specs=[pl.BlockSpec((tm, d), row), pl.BlockSpec((1, d), const),
                  pl.BlockSpec(w.shape, const), pl.BlockSpec(wvt.shape, const), pl.BlockSpec(bd.shape, const),
                  pl.BlockSpec(gain.shape, const),
                  pl.BlockSpec((tm, LANES), lambda i: (i % nt, 0)),
                  pl.BlockSpec((tm, LANES), lambda i: (i % nt, 0))],
        out_specs=[pl.BlockSpec((tm, nq), row), pl.BlockSpec((tm, nk), row), vt_spec],
        out_shape=[jax.ShapeDtypeStruct((n, nq), BF16), jax.ShapeDtypeStruct((n, nk), BF16), vt_shape],
        compiler_params=_cparams(("parallel",)),
        name="qkv_proj",
    )(x, g, w, wvt, bd, gain, cos, sin)


def _mla_kernel(x_ref, g_ref, wd_ref, qlg_ref, kvlg_ref, wuq_ref, wuk_ref, wuvt_ref, bd64_ref, bd32_ref,
                gq_ref, gk_ref, cos_ref, sin_ref, q_ref, k_ref, vt_ref, *, scale):
    xn = _rms_rows(x_ref[...], g_ref[...]).astype(BF16)
    c = jnp.dot(xn, wd_ref[...], preferred_element_type=F32)
    cq = _rms_rows(c[:, :B_Q_LORA], qlg_ref[...]).astype(BF16)
    ckv = _rms_rows(c[:, B_Q_LORA:B_Q_LORA + B_KV_LORA], kvlg_ref[...]).astype(BF16)
    q = jnp.dot(cq, wuq_ref[...], preferred_element_type=F32)
    kn = jnp.dot(ckv, wuk_ref[...], preferred_element_type=F32)
    bd64, bd32 = bd64_ref[...], bd32_ref[...]
    cos, sin = cos_ref[...], sin_ref[...]
    gq, gk = gq_ref[...], gk_ref[...]
    hw = B_HEADS * B_NOPE
    qn = _head_rms(q[:, :hw], bd64, gq[:, :hw], B_NOPE) * scale
    qr = _rope_slabs(_head_rms(q[:, hw:], bd32, gq[:, hw:], B_ROPE), cos, sin, B_ROPE // 2) * scale
    kn = _head_rms(kn, bd64, gk[:, :hw], B_NOPE)
    kr_raw = c[:, B_Q_LORA + B_KV_LORA:]
    kr2 = _head_rms(jnp.concatenate([kr_raw, kr_raw], axis=-1), bd32,
                    jnp.concatenate([gk[:, hw:], gk[:, hw:]], axis=-1), B_ROPE)
    kr = _rope_slabs(kr2[:, :LANES], cos, sin, B_ROPE // 2).astype(BF16)
    qn, qr, kn = qn.astype(BF16), qr.astype(BF16), kn.astype(BF16)
    for p in range(B_HEADS // 2):
        sl = slice(p * LANES, (p + 1) * LANES)
        q_ref[:, 2 * p * LANES:(2 * p + 1) * LANES] = qn[:, sl]
        q_ref[:, (2 * p + 1) * LANES:(2 * p + 2) * LANES] = qr[:, sl]
        k_ref[:, 2 * p * LANES:(2 * p + 1) * LANES] = kn[:, sl]
        k_ref[:, (2 * p + 1) * LANES:(2 * p + 2) * LANES] = kr
    vt = lax.dot_general(wuvt_ref[...], ckv, _NT, preferred_element_type=F32)
    _store_vt(vt_ref, vt, heads=B_HEADS, dv=B_V)


def _mla_proj(x, g, wd, qlg, kvlg, wuq, wuk, wuvt, bd64, bd32, gq, gk, cos, sin, *, scale, seq, tm):
    n, d = x.shape
    nt = seq // tm
    row = lambda i: (i, 0)
    const = lambda i: (0, 0)
    full = lambda a: pl.BlockSpec(a.shape, const)
    wq = (B_HEADS // 2) * 2 * LANES
    vt_spec, vt_shape = _vt_out(n, seq, tm, B_HEADS * (B_V + V_ONES))
    return pl.pallas_call(
        functools.partial(_mla_kernel, scale=scale),
        grid=(n // tm,),
        in_specs=[pl.BlockSpec((tm, d), row), full(g), full(wd), full(qlg), full(kvlg), full(wuq), full(wuk),
                  full(wuvt), full(bd64), full(bd32), full(gq), full(gk),
                  pl.BlockSpec((tm, LANES), lambda i: (i % nt, 0)),
                  pl.BlockSpec((tm, LANES), lambda i: (i % nt, 0))],
        out_specs=[pl.BlockSpec((tm, wq), row), pl.BlockSpec((tm, wq), row), vt_spec],
        out_shape=[jax.ShapeDtypeStruct((n, wq), BF16), jax.ShapeDtypeStruct((n, wq), BF16), vt_shape],
        compiler_params=_cparams(("parallel",)),
        name="mla_proj",
    )(x, g, wd, qlg, kvlg, wuq, wuk, wuvt, bd64, bd32, gq, gk, cos, sin)


def _attn_kernel(*refs, mode, qw, dv, nv, tq, nk, band, pp, share):
    it = iter(refs)
    q_ref, k_ref, vt_ref = next(it), next(it), next(it)
    bias_ref = next(it) if band is not None else None
    if mode == "diff":
        lam_ref, subg_ref = next(it), next(it)
    o_ref = next(it)
    qm_sc, m_sc, acc_sc, sa_sc, sb_sc = next(it), next(it), next(it), next(it), next(it)

    blk = ATT_BLK
    hv = dv + V_ONES
    nsub = tq // blk
    chains = [(pi, r, h) for pi in range(pp) for r in range(nsub) for h in range(2)]
    qi = pl.program_id(2)

    lane = lax.broadcasted_iota(jnp.int32, (1, qw), 1)
    if qw == LANES:
        masks = (lane < HEAD_DIM, lane >= HEAD_DIM)
    else:
        masks = ((lane < HEAD_DIM) | ((lane >= LANES) & (lane < LANES + B_ROPE)),
                 ((lane >= HEAD_DIM) & (lane < LANES)) | ((lane >= LANES + B_ROPE) & (lane < LANES + 2 * B_ROPE)))
    for ci, (pi, r, h) in enumerate(chains):
        qr = q_ref[0, r * blk:(r + 1) * blk, pi * qw:(pi + 1) * qw]
        qm_sc[ci] = jnp.where(masks[h], qr, jnp.zeros_like(qr))
    m_sc[...] = jnp.full(m_sc.shape, NEG_BIG, F32)
    acc_sc[...] = jnp.zeros(acc_sc.shape, F32)

    def score_chain(ci, c, k, s_sc):
        pi = 0 if share else chains[ci][0]
        st = lax.dot_general(k[:, pi * qw:(pi + 1) * qw], qm_sc[ci], _NT,
                             preferred_element_type=F32)
        if band is not None:
            rel = c - (qi * nsub + chains[ci][1]) + band + 1
            st = st + bias_ref[jnp.clip(rel, 0, 2 * band + 2)]
        s_sc[ci] = st

    def consume_chain(ci, vt, s_sc):
        st = s_sc[ci]
        m_old = m_sc[ci]
        m_new = jnp.maximum(m_old, jnp.max(st, axis=0, keepdims=True))
        p = jnp.exp2(st - m_new).astype(BF16)
        alpha = jnp.exp2(m_old - m_new)
        m_sc[ci] = m_new
        pi, _, h = chains[ci]
        voff = ((0 if share else pi * nv) + (h if nv == 2 else 0)) * hv
        pv = jnp.dot(vt[voff:voff + hv], p, preferred_element_type=F32)
        acc_sc[ci] = acc_sc[ci] * alpha + pv

    def step(c_cur, s_cur, c_next=None, s_next=None):
        vt = vt_ref[0, c_cur]
        if c_next is not None:
            k = k_ref[0, pl.ds(pl.multiple_of(c_next * blk, blk), blk), :]
        for ci in range(len(chains)):
            if c_next is not None:
                score_chain(ci, c_next, k, s_next)
            consume_chain(ci, vt, s_cur)

    if band is None:
        lo, hi = 0, nk
    else:
        lo, hi = jnp.maximum(qi * nsub - band, 0), jnp.minimum((qi + 1) * nsub + band, nk)
    npairs = (hi - lo - 1) // 2

    def body(i, carry):
        c = lo + 2 * i
        step(c, sa_sc, c + 1, sb_sc)
        step(c + 1, sb_sc, c + 2, sa_sc)
        return carry

    k0 = k_ref[0, pl.ds(pl.multiple_of(lo * blk, blk), blk), :]
    for ci in range(len(chains)):
        score_chain(ci, lo, k0, sa_sc)
    lax.fori_loop(0, npairs, body, 0)
    c_tail = lo + 2 * npairs
    if band is None:
        if (hi - lo) % 2 == 0:
            step(c_tail, sa_sc, c_tail + 1, sb_sc)
            step(c_tail + 1, sb_sc)
        else:
            step(c_tail, sa_sc)
    else:
        two_left = (hi - c_tail) == 2

        @pl.when(two_left)
        def _():
            step(c_tail, sa_sc, c_tail + 1, sb_sc)
            step(c_tail + 1, sb_sc)

        @pl.when(jnp.logical_not(two_left))
        def _():
            step(c_tail, sa_sc)

    for pi, r in [(pi, r) for pi in range(pp) for r in range(nsub)]:
        a, b = acc_sc[2 * (pi * nsub + r)], acc_sc[2 * (pi * nsub + r) + 1]
        oa = a[:dv] / a[dv:dv + 1]
        ob = b[:dv] / b[dv:dv + 1]
        if mode == "pair":
            o = jnp.concatenate([oa, ob], axis=0).T
        else:
            o = (oa - lam_ref[0, 0] * ob).T
            o = _rms_rows(o, subg_ref[...]) * (1.0 - lam_ref[0, 1])
        o_ref[0, r * blk:(r + 1) * blk, pi * LANES:(pi + 1) * LANES] = o.astype(o_ref.dtype)


def _attention(q, k, vt, *, mode, qw, dv, nv, share, tq, bias=None, band=None, lam=None, subg=None):
    b, s, cq = q.shape
    pairs = cq // qw
    nk = s // ATT_BLK
    hv = dv + V_ONES
    pp = share if share else max(1, min(2, ATT_CHAINS // (2 * (tq // ATT_BLK))))
    kpp = 1 if share else pp
    in_specs = [pl.BlockSpec((1, tq, pp * qw), lambda bi, p, i: (bi, i, p)),
                pl.BlockSpec((1, s, kpp * qw), lambda bi, p, i: (bi, 0, p)),
                pl.BlockSpec((1, nk, kpp * nv * hv, ATT_BLK), lambda bi, p, i: (bi, 0, p, 0))]
    args = [q, k, vt]
    if band is not None:
        in_specs.append(pl.BlockSpec(bias.shape, lambda bi, p, i: (0, 0, 0)))
        args.append(bias)
    if mode == "diff":
        in_specs.append(pl.BlockSpec(memory_space=pltpu.SMEM))
        in_specs.append(pl.BlockSpec(subg.shape, lambda bi, p, i: (0, 0)))
        args += [lam, subg]
    nchains = 2 * pp * (tq // ATT_BLK)
    return pl.pallas_call(
        functools.partial(_attn_kernel, mode=mode, qw=qw, dv=dv, nv=nv, tq=tq, nk=nk, band=band, pp=pp,
                          share=bool(share)),
        grid=(b, pairs // pp, s // tq),
        in_specs=in_specs,
        out_specs=pl.BlockSpec((1, tq, pp * LANES), lambda bi, p, i: (bi, i, p)),
        out_shape=jax.ShapeDtypeStruct((b, s, pairs * LANES), BF16),
        scratch_shapes=[pltpu.VMEM((nchains, ATT_BLK, qw), BF16), pltpu.VMEM((nchains, 1, ATT_BLK), F32),
                        pltpu.VMEM((nchains, hv, ATT_BLK), F32),
                        pltpu.VMEM((nchains, ATT_BLK, ATT_BLK), F32), pltpu.VMEM((nchains, ATT_BLK, ATT_BLK), F32)],
        compiler_params=_cparams(("parallel", "parallel", "arbitrary")),
        name="attention_" + mode,
    )(*args)


def _oproj_kernel(o_ref, w_ref, x_ref, g_ref, wr_ref, xnew_ref, xn_ref, aff_ref):
    xnew = x_ref[...] + jnp.dot(o_ref[...], w_ref[...], preferred_element_type=F32)
    xnew_ref[...] = xnew
    xn = _rms_rows(xnew, g_ref[...])
    xn_ref[...] = xn
    logits = lax.dot_general(wr_ref[...], xn.astype(BF16), _NT, preferred_element_type=F32)
    z = logits - jnp.max(logits, axis=0, keepdims=True)
    e = jnp.exp(z)
    aff_ref[...] = e / jnp.sum(e, axis=0, keepdims=True)


def _oproj(o, w, x, g, wr_t, *, tm):
    n, d = x.shape
    row = lambda i: (i, 0)
    const = lambda i: (0, 0)
    return pl.pallas_call(
        _oproj_kernel,
        grid=(n // tm,),
        in_specs=[pl.BlockSpec((tm, o.shape[1]), row), pl.BlockSpec(w.shape, const), pl.BlockSpec((tm, d), row),
                  pl.BlockSpec((1, d), const), pl.BlockSpec(wr_t.shape, const)],
        out_specs=[pl.BlockSpec((tm, d), row), pl.BlockSpec((tm, d), row),
                   pl.BlockSpec((N_EXPERTS, tm), lambda i: (0, i))],
        out_shape=[jax.ShapeDtypeStruct((n, d), F32), jax.ShapeDtypeStruct((n, d), F32),
                   jax.ShapeDtypeStruct((N_EXPERTS, n), F32)],
        compiler_params=_cparams(("parallel",)),
        name="oproj_router",
    )(o, w, x, g, wr_t)


def _ffn_kernel(xg_ref, wg_ref, wu_ref, wd_ref, gate_ref, y_ref):
    xg = xg_ref[0].astype(BF16)
    a = jnp.dot(xg, wg_ref[0, 0], preferred_element_type=F32)
    u = jnp.dot(xg, wu_ref[0, 0], preferred_element_type=F32)
    h = (a * jax.nn.sigmoid(a) * u).astype(BF16)
    y_ref[0] = jnp.dot(h, wd_ref[0, 0], preferred_element_type=F32) * gate_ref[0]


def _expert_ffn(xg, wg, wu, wd, gate, *, layer, tm):
    e, cap, d = xg.shape
    ff = wg.shape[3]
    return pl.pallas_call(
        _ffn_kernel,
        grid=(e, cap // tm),
        in_specs=[pl.BlockSpec((1, tm, d), lambda ei, i: (ei, i, 0)),
                  pl.BlockSpec((1, 1, d, ff), lambda ei, i: (layer, ei, 0, 0)),
                  pl.BlockSpec((1, 1, d, ff), lambda ei, i: (layer, ei, 0, 0)),
                  pl.BlockSpec((1, 1, ff, d), lambda ei, i: (layer, ei, 0, 0)),
                  pl.BlockSpec((1, tm, 1), lambda ei, i: (ei, i, 0))],
        out_specs=pl.BlockSpec((1, tm, d), lambda ei, i: (ei, i, 0)),
        out_shape=jax.ShapeDtypeStruct((e, cap, d), F32),
        compiler_params=_cparams(("parallel", "arbitrary")),
        name="expert_ffn",
    )(xg, wg, wu, wd, gate)


def _combine_kernel(offs_ref, x_ref, tok_hbm, z_hbm, o_ref, tokbuf, zbuf, sem, *, nsub):
    step = pl.program_id(0)
    blk0 = step * nsub
    c_lo = lax.shift_right_logical(offs_ref[blk0], SEG_ROWS_LOG2)
    c_hi = lax.shift_right_logical(offs_ref[blk0 + nsub] + (SEG_ROWS - 1), SEG_ROWS_LOG2)
    o_ref[...] = x_ref[...]

    def copies(c, slot):
        return (pltpu.make_async_copy(tok_hbm.at[c], tokbuf.at[slot], sem.at[0, slot]),
                pltpu.make_async_copy(z_hbm.at[pl.ds(pl.multiple_of(c * SEG_ROWS, SEG_ROWS), SEG_ROWS)],
                                      zbuf.at[slot], sem.at[1, slot]))

    for j in range(SEG_BUFS - 1):
        @pl.when(c_lo + j < c_hi)
        def _():
            for cp in copies(c_lo + j, j):
                cp.start()

    def body(c, carry):
        slot = lax.rem(c - c_lo, SEG_BUFS)
        for cp in copies(c, slot):
            cp.wait()
        ahead = c + (SEG_BUFS - 1)

        @pl.when(ahead < c_hi)
        def _():
            for cp in copies(ahead, lax.rem(ahead - c_lo, SEG_BUFS)):
                cp.start()

        tok = tokbuf[slot]
        z = zbuf[slot]
        zh = z.astype(BF16)
        zl = (z - zh.astype(F32)).astype(BF16)
        row0 = c * SEG_ROWS
        u_first, u_end = jnp.int32(0), jnp.int32(0)
        for u in range(nsub):
            u_first = u_first + (offs_ref[blk0 + u + 1] <= row0).astype(jnp.int32)
            u_end = u_end + (offs_ref[blk0 + u] < row0 + SEG_ROWS).astype(jnp.int32)

        def add_block(u, valid):
            base = jnp.where(valid, (blk0 + u) * SEG_TOK, -2 * SEG_TOK)
            tid = base + lax.broadcasted_iota(jnp.int32, (SEG_TOK, SEG_ROWS), 0)
            onehot = jnp.where(tok == tid, 1.0, 0.0).astype(BF16)
            upd = (jnp.dot(onehot, zh, preferred_element_type=F32)
                   + jnp.dot(onehot, zl, preferred_element_type=F32))
            start = pl.multiple_of(jnp.minimum(u, nsub - 1) * SEG_TOK, SEG_TOK)
            o_ref[pl.ds(start, SEG_TOK), :] += upd

        add_block(u_first, u_first < u_end)
        add_block(u_first + 1, u_first + 1 < u_end)

        def more(u, cr):
            add_block(u, True)
            return cr

        lax.fori_loop(u_first + 2, u_end, more, 0)
        return carry

    lax.fori_loop(c_lo, c_hi, body, 0)


def _combine(offs, x, tok, z):
    n, d = x.shape
    tn = _pick(n, SEG_STEP_TOK)
    return pl.pallas_call(
        functools.partial(_combine_kernel, nsub=tn // SEG_TOK),
        grid_spec=pltpu.PrefetchScalarGridSpec(
            num_scalar_prefetch=1,
            grid=(n // tn,),
            in_specs=[pl.BlockSpec((tn, d), lambda i, offs: (i, 0)),
                      pl.BlockSpec(memory_space=pl.ANY), pl.BlockSpec(memory_space=pl.ANY)],
            out_specs=pl.BlockSpec((tn, d), lambda i, offs: (i, 0)),
            scratch_shapes=[pltpu.VMEM((SEG_BUFS, 1, SEG_ROWS), jnp.int32),
                            pltpu.VMEM((SEG_BUFS, SEG_ROWS, d), F32),
                            pltpu.SemaphoreType.DMA((2, SEG_BUFS))]),
        out_shape=jax.ShapeDtypeStruct((n, d), F32),
        compiler_params=_cparams(("arbitrary",)),
        name="combine_segsum",
    )(offs, x, tok, z)


def _block_diag(hd):
    i = jnp.arange(MXU_DIM)
    return (i[:, None] // hd == i[None, :] // hd).astype(BF16)


def _rope_tables_std(seq, dim):
    half = dim // 2
    inv_freq = ROPE_THETA ** (-jnp.arange(half, dtype=F32) * 2.0 / dim)
    ang = jnp.arange(seq).astype(F32)[:, None] * inv_freq[None, :]
    cos, sin = jnp.cos(ang), jnp.sin(ang)
    reps = LANES // dim
    return (jnp.tile(jnp.concatenate([cos, cos], axis=-1), (1, reps)),
            jnp.tile(jnp.concatenate([-sin, sin], axis=-1), (1, reps)))


def _rope_tables_axial(seq):
    sub = HEAD_DIM // 2
    half = sub // 2
    inv_freq = ROPE_THETA ** (-jnp.arange(half, dtype=F32) * 2.0 / sub)
    n_rows = seq // GRID_W
    rows = jnp.repeat(jnp.arange(n_rows), GRID_W).astype(F32)
    cols = jnp.tile(jnp.arange(GRID_W), n_rows).astype(F32)
    ar, ac = rows[:, None] * inv_freq[None, :], cols[:, None] * inv_freq[None, :]
    cos = jnp.concatenate([jnp.cos(ar), jnp.cos(ar), jnp.cos(ac), jnp.cos(ac)], axis=-1)
    sin = jnp.concatenate([-jnp.sin(ar), jnp.sin(ar), -jnp.sin(ac), jnp.sin(ac)], axis=-1)
    return jnp.tile(cos, (1, 2)), jnp.tile(sin, (1, 2))


def _dilated_bias(band):
    t = ATT_BLK
    rel = jnp.arange(-band - 1, band + 2)[:, None, None] * t
    d = rel + jnp.arange(t)[None, :, None] - jnp.arange(t)[None, None, :]
    cnt = jnp.zeros(d.shape, F32)
    for window, dil in DILATED_BRANCHES:
        cnt = cnt + ((d % dil == 0) & (jnp.abs(d) <= window // 2)).astype(F32)
    return jnp.where(cnt > 0, jnp.log2(jnp.maximum(cnt, 1.0)), NEG_BIG)


def _tile_gain(g, reps):
    return jnp.tile(g.astype(F32), reps)[None, :]


def _pick(n, pref):
    t = min(n, pref)
    while n % t:
        t //= 2
    return t


def _mixer_a(x, b, s, g, w_qkv, q_g, k_g):
    nq = A_HEADS * HEAD_DIM
    nkv = A_KV_HEADS * HEAD_DIM
    wq, wk, wv = w_qkv[:, :nq], w_qkv[:, nq:nq + nkv], w_qkv[:, nq + nkv:]
    wk2 = jnp.repeat(wk.reshape(D_MODEL, A_KV_HEADS, 1, HEAD_DIM), 2, axis=2).reshape(D_MODEL, -1)
    w = jnp.concatenate([wq, wk2], axis=1).astype(BF16)
    nk = 2 * nkv
    gain = jnp.concatenate([_tile_gain(q_g, nq // HEAD_DIM), _tile_gain(k_g, nk // HEAD_DIM)], axis=1)
    cos, sin = _rope_tables_axial(s)
    q, k, vt = _qkv_proj(x, g, w, wv.T.astype(BF16), gain, cos, sin, _block_diag(HEAD_DIM), nq=nq, nk=nk,
                         half=HEAD_DIM // 4, q_scale=HEAD_DIM ** -0.5 * LOG2E, v_heads=A_KV_HEADS, dv=HEAD_DIM,
                         seq=s, tm=_pick(s, ROW_TILE))
    group_pairs = (A_HEADS // A_KV_HEADS) // 2
    return _attention(q.reshape(b, s, -1), k.reshape(b, s, -1), vt, mode="pair", qw=LANES, dv=HEAD_DIM, nv=1,
                      share=group_pairs, tq=_pick(s, ATT_TQ // group_pairs))


def _mixer_b(x, b, s, g, w_down, q_lat_g, kv_lat_g, w_uq, w_ukv, q_nope_g, q_rope_g, k_nope_g, k_rope_g):
    zpad = lambda a, n: jnp.concatenate([a, jnp.zeros(a.shape[:-1] + (n,), a.dtype)], axis=-1)
    w_kr = w_down[:, B_Q_LORA + B_KV_LORA:]
    wd = jnp.concatenate([w_down[:, :B_Q_LORA + B_KV_LORA], zpad(jnp.concatenate([w_kr, w_kr], axis=1), 2 * B_ROPE)],
                         axis=1).astype(BF16)
    uq = w_uq.reshape(B_Q_LORA, B_HEADS, B_NOPE + B_ROPE)
    uq_n = uq[:, :, :B_NOPE].reshape(B_Q_LORA, -1)
    uq_r = zpad(uq[:, :, B_NOPE:].reshape(B_Q_LORA, B_HEADS // 2, 2 * B_ROPE), LANES - 2 * B_ROPE).reshape(B_Q_LORA, -1)
    wuq = jnp.concatenate([uq_n, uq_r], axis=1).astype(BF16)
    ukv = w_ukv.reshape(B_KV_LORA, B_HEADS, B_NOPE + B_V)
    wuk = ukv[:, :, :B_NOPE].reshape(B_KV_LORA, -1).astype(BF16)
    wuvt = ukv[:, :, B_NOPE:].reshape(B_KV_LORA, -1).T.astype(BF16)
    gq = jnp.concatenate([_tile_gain(q_nope_g, B_HEADS), _tile_gain(q_rope_g, B_HEADS * B_NOPE // B_ROPE)], axis=1)
    gk = jnp.concatenate([_tile_gain(k_nope_g, B_HEADS), _tile_gain(k_rope_g, LANES // B_ROPE)], axis=1)
    cos, sin = _rope_tables_std(s, B_ROPE)
    q, k, vt = _mla_proj(x, g, wd, q_lat_g.astype(F32)[None, :], kv_lat_g.astype(F32)[None, :], wuq, wuk, wuvt,
                         _block_diag(B_NOPE), _block_diag(B_ROPE), gq, gk, cos, sin,
                         scale=(B_NOPE + B_ROPE) ** -0.5 * LOG2E, seq=s, tm=_pick(s, ROW_TILE))
    return _attention(q.reshape(b, s, -1), k.reshape(b, s, -1), vt, mode="pair", qw=2 * LANES, dv=B_V, nv=2,
                      share=0, tq=_pick(s, ATT_TQ))


def _mixer_c(x, b, s, g, w_qkv, q_g, k_g, lq1, lk1, lq2, lk2, subln_g, layer_idx):
    nq = 2 * C_HEADS * HEAD_DIM
    gain = jnp.concatenate([_tile_gain(q_g, nq // HEAD_DIM), _tile_gain(k_g, nq // HEAD_DIM)], axis=1)
    cos, sin = _rope_tables_std(s, HEAD_DIM)
    q, k, vt = _qkv_proj(x, g, w_qkv[:, :2 * nq].astype(BF16), w_qkv[:, 2 * nq:].T.astype(BF16), gain, cos, sin,
                         _block_diag(HEAD_DIM), nq=nq, nk=nq, half=HEAD_DIM // 2, q_scale=HEAD_DIM ** -0.5 * LOG2E,
                         v_heads=C_HEADS, dv=2 * HEAD_DIM, seq=s, tm=_pick(s, ROW_TILE))
    lam_init = LAMBDA_INIT_BASE - LAMBDA_INIT_AMP * math.exp(-LAMBDA_INIT_RATE * layer_idx)
    lam = (jnp.exp(jnp.sum(lq1.astype(F32) * lk1.astype(F32))) - jnp.exp(jnp.sum(lq2.astype(F32) * lk2.astype(F32)))
           + lam_init)
    lam_arr = jnp.stack([lam, jnp.asarray(lam_init, F32)]).reshape(1, 2).astype(F32)
    return _attention(q.reshape(b, s, -1), k.reshape(b, s, -1), vt, mode="diff", qw=LANES, dv=2 * HEAD_DIM, nv=1,
                      share=0, tq=_pick(s, ATT_TQ), lam=lam_arr,
                      subg=subln_g.astype(F32)[None, :])


def _mixer_d(x, b, s, g, w_qkv, q_g, k_g):
    nq = D_HEADS * HEAD_DIM
    gain = jnp.concatenate([_tile_gain(q_g, D_HEADS), _tile_gain(k_g, D_HEADS)], axis=1)
    cos, sin = _rope_tables_std(s, HEAD_DIM)
    q, k, vt = _qkv_proj(x, g, w_qkv[:, :2 * nq].astype(BF16), w_qkv[:, 2 * nq:].T.astype(BF16), gain, cos, sin,
                         _block_diag(HEAD_DIM), nq=nq, nk=nq, half=HEAD_DIM // 2, q_scale=HEAD_DIM ** -0.5 * LOG2E,
                         v_heads=D_HEADS, dv=HEAD_DIM, seq=s, tm=_pick(s, ROW_TILE))
    reach = max(w // 2 for w, _ in DILATED_BRANCHES)
    band = -(-reach // ATT_BLK)
    tq = _pick(s, ATT_TQ if ATT_TQ // ATT_BLK + 2 * band >= s // ATT_BLK else 2 * ATT_BLK)
    return _attention(q.reshape(b, s, -1), k.reshape(b, s, -1), vt, mode="pair", qw=LANES, dv=HEAD_DIM, nv=2,
                      share=0, tq=tq, bias=_dilated_bias(band), band=band)


def _ec_ffn(xnew, xn, aff, wg, wu, wd, layer):
    n, d = xnew.shape
    cap = (EC_CAPACITY * n) // N_EXPERTS
    gate, idx = lax.top_k(aff, cap)
    y = _expert_ffn(xn[idx], wg, wu, wd, gate[..., None], layer=layer, tm=_pick(cap, ROW_TILE))
    flat = idx.reshape(-1)
    tok_sorted, order = lax.sort_key_val(flat, jnp.arange(flat.size, dtype=jnp.int32))
    bounds = jnp.arange(0, n + 1, SEG_TOK, dtype=jnp.int32)
    offs = jnp.sum((tok_sorted[None, :] < bounds[:, None]).astype(jnp.int32), axis=1)
    return _combine(offs, xnew, tok_sorted.reshape(-1, 1, SEG_ROWS), y.reshape(-1, d)[order])


def kernel(x_prompt, x_sample, norm_mix_g, norm_ffn_g, a_w_qkv, a_q_norm_g, a_k_norm_g, a_w_o, b_w_down, b_q_lat_norm_g, b_kv_lat_norm_g, b_w_uq, b_w_ukv, b_q_nope_norm_g, b_q_rope_norm_g, b_k_nope_norm_g, b_k_rope_norm_g, b_w_o, c_w_qkv, c_q_norm_g, c_k_norm_g, c_lambda_q1, c_lambda_k1, c_lambda_q2, c_lambda_k2, c_subln_g, c_w_o, d_w_qkv, d_q_norm_g, d_k_norm_g, d_w_o, ec_w_router, ec_w_gate, ec_w_up, ec_w_down):
    depth = norm_mix_g.shape[0]
    wg_all, wu_all, wd_all = ec_w_gate.astype(BF16), ec_w_up.astype(BF16), ec_w_down.astype(BF16)

    def trunk(x3):
        b, s, d = x3.shape
        x = x3.reshape(b * s, d)
        for i in range(depth):
            m, j = i % N_MIXERS, i // N_MIXERS
            g = norm_mix_g[i].astype(F32)[None, :]
            if m == 0:
                o, w_o = _mixer_a(x, b, s, g, a_w_qkv[j], a_q_norm_g[j], a_k_norm_g[j]), a_w_o[j]
            elif m == 1:
                o = _mixer_b(x, b, s, g, b_w_down[j], b_q_lat_norm_g[j], b_kv_lat_norm_g[j], b_w_uq[j], b_w_ukv[j],
                             b_q_nope_norm_g[j], b_q_rope_norm_g[j], b_k_nope_norm_g[j], b_k_rope_norm_g[j])
                w_o = b_w_o[j]
            elif m == 2:
                o = _mixer_c(x, b, s, g, c_w_qkv[j], c_q_norm_g[j], c_k_norm_g[j], c_lambda_q1[j], c_lambda_k1[j],
                             c_lambda_q2[j], c_lambda_k2[j], c_subln_g[j], i)
                w_o = c_w_o[j]
            else:
                o, w_o = _mixer_d(x, b, s, g, d_w_qkv[j], d_q_norm_g[j], d_k_norm_g[j]), d_w_o[j]
            xnew, xn, aff = _oproj(o.reshape(b * s, -1), w_o.astype(BF16), x, norm_ffn_g[i].astype(F32)[None, :],
                                   ec_w_router[i].T.astype(BF16), tm=_pick(b * s, ROW_TILE))
            x = _ec_ffn(xnew, xn, aff, wg_all, wu_all, wd_all, i)
        return x.reshape(b, s, d)

    return (trunk(x_prompt), trunk(x_sample))
```

```python
import functools
import math

import jax
import jax.numpy as jnp
from jax import lax
from jax.experimental import pallas as pl
from jax.experimental.pallas import tpu as pltpu

F32 = jnp.float32
BF16 = jnp.bfloat16

D_MODEL = 1024
HEAD_DIM = 64
GRID_W = 64
ROPE_THETA = 10000.0
RMS_EPS = 1e-6
N_MIXERS = 4
A_HEADS, A_KV_HEADS = 16, 4
B_HEADS, B_Q_LORA, B_KV_LORA, B_NOPE, B_ROPE, B_V = 16, 384, 256, 64, 32, 64
C_HEADS = 8
LAMBDA_INIT_BASE, LAMBDA_INIT_AMP, LAMBDA_INIT_RATE = 0.8, 0.6, 0.3
D_HEADS = 16
DILATED_BRANCHES = ((128, 1), (512, 4), (2048, 16))
N_EXPERTS = 16
EC_CAPACITY = 2

LANES = 128
MXU_DIM = 256
ROW_TILE = 2 * MXU_DIM
VMEM_LIMIT = 56 * 1024 * 1024
V_ONES = 16
ATT_BLK = MXU_DIM
ATT_CHAINS = 32
ATT_TQ = (ATT_CHAINS // 2) * ATT_BLK
SEG_TOK = MXU_DIM
SEG_ROWS_LOG2 = 8
SEG_ROWS = 1 << SEG_ROWS_LOG2
SEG_STEP_TOK = 8 * SEG_TOK
SEG_BUFS = 4
NEG_BIG = -1e30
LOG2E = math.log2(math.e)


def _cparams(sem):
    return pltpu.CompilerParams(dimension_semantics=sem, vmem_limit_bytes=VMEM_LIMIT)


_NT = (((1,), (1,)), ((), ()))


def _rms_rows(x, g):
    return x * lax.rsqrt(jnp.mean(x * x, axis=-1, keepdims=True) + RMS_EPS) * g


def _head_rms(t, bd, gain, hd):
    outs = []
    for j in range(t.shape[1] // MXU_DIM):
        tj = t[:, j * MXU_DIM:(j + 1) * MXU_DIM]
        ss = jnp.dot((tj * tj).astype(BF16), bd, preferred_element_type=F32)
        outs.append(tj * lax.rsqrt(ss * (1.0 / hd) + RMS_EPS))
    y = outs[0] if len(outs) == 1 else jnp.concatenate(outs, axis=-1)
    return y * gain


def _rope_slabs(y, cos, sin, half):
    lane = lax.broadcasted_iota(jnp.int32, (1, LANES), 1)
    first = (lane % (2 * half)) < half
    outs = []
    for j in range(y.shape[1] // LANES):
        yj = y[:, j * LANES:(j + 1) * LANES]
        swapped = jnp.where(first, pltpu.roll(yj, LANES - half, 1), pltpu.roll(yj, half, 1))
        outs.append(yj * cos + swapped * sin)
    return outs[0] if len(outs) == 1 else jnp.concatenate(outs, axis=-1)


def _store_vt(vt_ref, vt, *, heads, dv):
    hv = dv + V_ONES
    ones = jnp.ones((V_ONES, ATT_BLK), BF16)
    vt = vt.astype(BF16)
    for j in range(vt.shape[1] // ATT_BLK):
        for h in range(heads):
            vt_ref[0, j, h * hv:h * hv + dv, :] = vt[h * dv:(h + 1) * dv, j * ATT_BLK:(j + 1) * ATT_BLK]
            vt_ref[0, j, h * hv + dv:(h + 1) * hv, :] = ones


def _qkv_kernel(x_ref, g_ref, w_ref, wvt_ref, bd_ref, gain_ref, cos_ref, sin_ref, q_ref, k_ref, vt_ref,
                *, nq, half, q_scale, v_heads, dv):
    xn = _rms_rows(x_ref[...], g_ref[...]).astype(BF16)
    y = jnp.dot(xn, w_ref[...], preferred_element_type=F32)
    bd = bd_ref[...]
    cos, sin = cos_ref[...], sin_ref[...]
    gain = gain_ref[...]
    q = _rope_slabs(_head_rms(y[:, :nq], bd, gain[:, :nq], HEAD_DIM), cos, sin, half) * q_scale
    k = _rope_slabs(_head_rms(y[:, nq:], bd, gain[:, nq:], HEAD_DIM), cos, sin, half)
    q_ref[...] = q.astype(BF16)
    k_ref[...] = k.astype(BF16)
    vt = lax.dot_general(wvt_ref[...], xn, _NT, preferred_element_type=F32)
    _store_vt(vt_ref, vt, heads=v_heads, dv=dv)


def _vt_out(n, seq, tm, vr):
    nt = seq // tm
    spec = pl.BlockSpec((1, tm // ATT_BLK, vr, ATT_BLK), lambda i: (i // nt, i % nt, 0, 0))
    return spec, jax.ShapeDtypeStruct((n // seq, seq // ATT_BLK, vr, ATT_BLK), BF16)


def _qkv_proj(x, g, w, wvt, gain, cos, sin, bd, *, nq, nk, half, q_scale, v_heads, dv, seq, tm):
    n, d = x.shape
    nt = seq // tm
    row = lambda i: (i, 0)
    const = lambda i: (0, 0)
    vt_spec, vt_shape = _vt_out(n, seq, tm, v_heads * (dv + V_ONES))
    return pl.pallas_call(
        functools.partial(_qkv_kernel, nq=nq, half=half, q_scale=q_scale, v_heads=v_heads, dv=dv),
        grid=(n // tm,),
        in_specs=[pl.BlockSpec((tm, d), row), pl.BlockSpec((1, d), const),
                  pl.BlockSpec(w.shape, const), pl.BlockSpec(wvt.shape, const), pl.BlockSpec(bd.shape, const),
                  pl.BlockSpec(gain.shape, const),
                  pl.BlockSpec((tm, LANES), lambda i: (i % nt, 0)),
                  pl.BlockSpec((tm, LANES), lambda i: (i % nt, 0))],
        out_specs=[pl.BlockSpec((tm, nq), row), pl.BlockSpec((tm, nk), row), vt_spec],
        out_shape=[jax.ShapeDtypeStruct((n, nq), BF16), jax.ShapeDtypeStruct((n, nk), BF16), vt_shape],
        compiler_params=_cparams(("parallel",)),
        name="qkv_proj",
    )(x, g, w, wvt, bd, gain, cos, sin)


def _mla_kernel(x_ref, g_ref, wd_ref, qlg_ref, kvlg_ref, wuq_ref, wuk_ref, wuvt_ref, bd64_ref, bd32_ref,
                gq_ref, gk_ref, cos_ref, sin_ref, q_ref, k_ref, vt_ref, *, scale):
    xn = _rms_rows(x_ref[...], g_ref[...]).astype(BF16)
    c = jnp.dot(xn, wd_ref[...], preferred_element_type=F32)
    cq = _rms_rows(c[:, :B_Q_LORA], qlg_ref[...]).astype(BF16)
    ckv = _rms_rows(c[:, B_Q_LORA:B_Q_LORA + B_KV_LORA], kvlg_ref[...]).astype(BF16)
    q = jnp.dot(cq, wuq_ref[...], preferred_element_type=F32)
    kn = jnp.dot(ckv, wuk_ref[...], preferred_element_type=F32)
    bd64, bd32 = bd64_ref[...], bd32_ref[...]
    cos, sin = cos_ref[...], sin_ref[...]
    gq, gk = gq_ref[...], gk_ref[...]
    hw = B_HEADS * B_NOPE
    qn = _head_rms(q[:, :hw], bd64, gq[:, :hw], B_NOPE) * scale
    qr = _rope_slabs(_head_rms(q[:, hw:], bd32, gq[:, hw:], B_ROPE), cos, sin, B_ROPE // 2) * scale
    kn = _head_rms(kn, bd64, gk[:, :hw], B_NOPE)
    kr_raw = c[:, B_Q_LORA + B_KV_LORA:]
    kr2 = _head_rms(jnp.concatenate([kr_raw, kr_raw], axis=-1), bd32,
                    jnp.concatenate([gk[:, hw:], gk[:, hw:]], axis=-1), B_ROPE)
    kr = _rope_slabs(kr2[:, :LANES], cos, sin, B_ROPE // 2).astype(BF16)
    qn, qr, kn = qn.astype(BF16), qr.astype(BF16), kn.astype(BF16)
    for p in range(B_HEADS // 2):
        sl = slice(p * LANES, (p + 1) * LANES)
        q_ref[:, 2 * p * LANES:(2 * p + 1) * LANES] = qn[:, sl]
        q_ref[:, (2 * p + 1) * LANES:(2 * p + 2) * LANES] = qr[:, sl]
        k_ref[:, 2 * p * LANES:(2 * p + 1) * LANES] = kn[:, sl]
        k_ref[:, (2 * p + 1) * LANES:(2 * p + 2) * LANES] = kr
    vt = lax.dot_general(wuvt_ref[...], ckv, _NT, preferred_element_type=F32)
    _store_vt(vt_ref, vt, heads=B_HEADS, dv=B_V)


def _mla_proj(x, g, wd, qlg, kvlg, wuq, wuk, wuvt, bd64, bd32, gq, gk, cos, sin, *, scale, seq, tm):
    n, d = x.shape
    nt = seq // tm
    row = lambda i: (i, 0)
    const = lambda i: (0, 0)
    full = lambda a: pl.BlockSpec(a.shape, const)
    wq = (B_HEADS // 2) * 2 * LANES
    vt_spec, vt_shape = _vt_out(n, seq, tm, B_HEADS * (B_V + V_ONES))
    return pl.pallas_call(
        functools.partial(_mla_kernel, scale=scale),
        grid=(n // tm,),
        in_specs=[pl.BlockSpec((tm, d), row), full(g), full(wd), full(qlg), full(kvlg), full(wuq), full(wuk),
                  full(wuvt), full(bd64), full(bd32), full(gq), full(gk),
                  pl.BlockSpec((tm, LANES), lambda i: (i % nt, 0)),
                  pl.BlockSpec((tm, LANES), lambda i: (i % nt, 0))],
        out_specs=[pl.BlockSpec((tm, wq), row), pl.BlockSpec((tm, wq), row), vt_spec],
        out_shape=[jax.ShapeDtypeStruct((n, wq), BF16), jax.ShapeDtypeStruct((n, wq), BF16), vt_shape],
        compiler_params=_cparams(("parallel",)),
        name="mla_proj",
    )(x, g, wd, qlg, kvlg, wuq, wuk, wuvt, bd64, bd32, gq, gk, cos, sin)


def _attn_kernel(*refs, mode, qw, dv, nv, tq, nk, band, pp, share):
    it = iter(refs)
    q_ref, k_ref, vt_ref = next(it), next(it), next(it)
    bias_ref = next(it) if band is not None else None
    if mode == "diff":
        lam_ref, subg_ref = next(it), next(it)
    o_ref = next(it)
    qm_sc, m_sc, acc_sc, sa_sc, sb_sc = next(it), next(it), next(it), next(it), next(it)

    blk = ATT_BLK
    hv = dv + V_ONES
    nsub = tq // blk
    chains = [(pi, r, h) for pi in range(pp) for r in range(nsub) for h in range(2)]
    qi = pl.program_id(2)

    lane = lax.broadcasted_iota(jnp.int32, (1, qw), 1)
    if qw == LANES:
        masks = (lane < HEAD_DIM, lane >= HEAD_DIM)
    else:
        masks = ((lane < HEAD_DIM) | ((lane >= LANES) & (lane < LANES + B_ROPE)),
                 ((lane >= HEAD_DIM) & (lane < LANES)) | ((lane >= LANES + B_ROPE) & (lane < LANES + 2 * B_ROPE)))
    for ci, (pi, r, h) in enumerate(chains):
        qr = q_ref[0, r * blk:(r + 1) * blk, pi * qw:(pi + 1) * qw]
        qm_sc[ci] = jnp.where(masks[h], qr, jnp.zeros_like(qr))
    m_sc[...] = jnp.full(m_sc.shape, NEG_BIG, F32)
    acc_sc[...] = jnp.zeros(acc_sc.shape, F32)

    def score_chain(ci, c, k, s_sc):
        pi = 0 if share else chains[ci][0]
        st = lax.dot_general(k[:, pi * qw:(pi + 1) * qw], qm_sc[ci], _NT,
                             preferred_element_type=F32)
        if band is not None:
            rel = c - (qi * nsub + chains[ci][1]) + band + 1
            st = st + bias_ref[jnp.clip(rel, 0, 2 * band + 2)]
        s_sc[ci] = st

    def consume_chain(ci, vt, s_sc):
        st = s_sc[ci]
        m_old = m_sc[ci]
        m_new = jnp.maximum(m_old, jnp.max(st, axis=0, keepdims=True))
        p = jnp.exp2(st - m_new).astype(BF16)
        alpha = jnp.exp2(m_old - m_new)
        m_sc[ci] = m_new
        pi, _, h = chains[ci]
        voff = ((0 if share else pi * nv) + (h if nv == 2 else 0)) * hv
        pv = jnp.dot(vt[voff:voff + hv], p, preferred_element_type=F32)
        acc_sc[ci] = acc_sc[ci] * alpha + pv

    def step(c_cur, s_cur, c_next=None, s_next=None):
        vt = vt_ref[0, c_cur]
        if c_next is not None:
            k = k_ref[0, pl.ds(pl.multiple_of(c_next * blk, blk), blk), :]
        for ci in range(len(chains)):
            if c_next is not None:
                score_chain(ci, c_next, k, s_next)
            consume_chain(ci, vt, s_cur)

    if band is None:
        lo, hi = 0, nk
    else:
        lo, hi = jnp.maximum(qi * nsub - band, 0), jnp.minimum((qi + 1) * nsub + band, nk)
    npairs = (hi - lo - 1) // 2

    def body(i, carry):
        c = lo + 2 * i
        step(c, sa_sc, c + 1, sb_sc)
        step(c + 1, sb_sc, c + 2, sa_sc)
        return carry

    k0 = k_ref[0, pl.ds(pl.multiple_of(lo * blk, blk), blk), :]
    for ci in range(len(chains)):
        score_chain(ci, lo, k0, sa_sc)
    lax.fori_loop(0, npairs, body, 0)
    c_tail = lo + 2 * npairs
    if band is None:
        if (hi - lo) % 2 == 0:
            step(c_tail, sa_sc, c_tail + 1, sb_sc)
            step(c_tail + 1, sb_sc)
        else:
            step(c_tail, sa_sc)
    else:
        two_left = (hi - c_tail) == 2

        @pl.when(two_left)
        def _():
            step(c_tail, sa_sc, c_tail + 1, sb_sc)
            step(c_tail + 1, sb_sc)

        @pl.when(jnp.logical_not(two_left))
        def _():
            step(c_tail, sa_sc)

    for pi, r in [(pi, r) for pi in range(pp) for r in range(nsub)]:
        a, b = acc_sc[2 * (pi * nsub + r)], acc_sc[2 * (pi * nsub + r) + 1]
        oa = a[:dv] / a[dv:dv + 1]
        ob = b[:dv] / b[dv:dv + 1]
        if mode == "pair":
            o = jnp.concatenate([oa, ob], axis=0).T
        else:
            o = (oa - lam_ref[0, 0] * ob).T
            o = _rms_rows(o, subg_ref[...]) * (1.0 - lam_ref[0, 1])
        o_ref[0, r * blk:(r + 1) * blk, pi * LANES:(pi + 1) * LANES] = o.astype(o_ref.dtype)


def _attention(q, k, vt, *, mode, qw, dv, nv, share, tq, bias=None, band=None, lam=None, subg=None):
    b, s, cq = q.shape
    pairs = cq // qw
    nk = s // ATT_BLK
    hv = dv + V_ONES
    pp = share if share else max(1, min(2, ATT_CHAINS // (2 * (tq // ATT_BLK))))
    kpp = 1 if share else pp
    in_specs = [pl.BlockSpec((1, tq, pp * qw), lambda bi, p, i: (bi, i, p)),
                pl.BlockSpec((1, s, kpp * qw), lambda bi, p, i: (bi, 0, p)),
                pl.BlockSpec((1, nk, kpp * nv * hv, ATT_BLK), lambda bi, p, i: (bi, 0, p, 0))]
    args = [q, k, vt]
    if band is not None:
        in_specs.append(pl.BlockSpec(bias.shape, lambda bi, p, i: (0, 0, 0)))
        args.append(bias)
    if mode == "diff":
        in_specs.append(pl.BlockSpec(memory_space=pltpu.SMEM))
        in_specs.append(pl.BlockSpec(subg.shape, lambda bi, p, i: (0, 0)))
        args += [lam, subg]
    nchains = 2 * pp * (tq // ATT_BLK)
    return pl.pallas_call(
        functools.partial(_attn_kernel, mode=mode, qw=qw, dv=dv, nv=nv, tq=tq, nk=nk, band=band, pp=pp,
                          share=bool(share)),
        grid=(b, pairs // pp, s // tq),
        in_specs=in_specs,
        out_specs=pl.BlockSpec((1, tq, pp * LANES), lambda bi, p, i: (bi, i, p)),
        out_shape=jax.ShapeDtypeStruct((b, s, pairs * LANES), BF16),
        scratch_shapes=[pltpu.VMEM((nchains, ATT_BLK, qw), BF16), pltpu.VMEM((nchains, 1, ATT_BLK), F32),
                        pltpu.VMEM((nchains, hv, ATT_BLK), F32),
                        pltpu.VMEM((nchains, ATT_BLK, ATT_BLK), F32), pltpu.VMEM((nchains, ATT_BLK, ATT_BLK), F32)],
        compiler_params=_cparams(("parallel", "parallel", "arbitrary")),
        name="attention_" + mode,
    )(*args)


def _oproj_kernel(o_ref, w_ref, x_ref, g_ref, wr_ref, xnew_ref, xn_ref, aff_ref):
    xnew = x_ref[...] + jnp.dot(o_ref[...], w_ref[...], preferred_element_type=F32)
    xnew_ref[...] = xnew
    xn = _rms_rows(xnew, g_ref[...])
    xn_ref[...] = xn
    logits = lax.dot_general(wr_ref[...], xn.astype(BF16), _NT, preferred_element_type=F32)
    z = logits - jnp.max(logits, axis=0, keepdims=True)
    e = jnp.exp(z)
    aff_ref[...] = e / jnp.sum(e, axis=0, keepdims=True)


def _oproj(o, w, x, g, wr_t, *, tm):
    n, d = x.shape
    row = lambda i: (i, 0)
    const = lambda i: (0, 0)
    return pl.pallas_call(
        _oproj_kernel,
        grid=(n // tm,),
        in_specs=[pl.BlockSpec((tm, o.shape[1]), row), pl.BlockSpec(w.shape, const), pl.BlockSpec((tm, d), row),
                  pl.BlockSpec((1, d), const), pl.BlockSpec(wr_t.shape, const)],
        out_specs=[pl.BlockSpec((tm, d), row), pl.BlockSpec((tm, d), row),
                   pl.BlockSpec((N_EXPERTS, tm), lambda i: (0, i))],
        out_shape=[jax.ShapeDtypeStruct((n, d), F32), jax.ShapeDtypeStruct((n, d), F32),
                   jax.ShapeDtypeStruct((N_EXPERTS, n), F32)],
        compiler_params=_cparams(("parallel",)),
        name="oproj_router",
    )(o, w, x, g, wr_t)


def _ffn_kernel(xg_ref, wg_ref, wu_ref, wd_ref, gate_ref, y_ref):
    xg = xg_ref[0].astype(BF16)
    a = jnp.dot(xg, wg_ref[0, 0], preferred_element_type=F32)
    u = jnp.dot(xg, wu_ref[0, 0], preferred_element_type=F32)
    h = (a * jax.nn.sigmoid(a) * u).astype(BF16)
    y = jnp.dot(h, wd_ref[0, 0], preferred_element_type=F32)
    g_cols = jnp.broadcast_to(gate_ref[0], (LANES, y.shape[0])).T
    for j in range(y.shape[1] // LANES):
        y_ref[0, :, j * LANES:(j + 1) * LANES] = y[:, j * LANES:(j + 1) * LANES] * g_cols


def _expert_ffn(xg, wg, wu, wd, gate, *, layer, tm):
    e, cap, d = xg.shape
    ff = wg.shape[3]
    return pl.pallas_call(
        _ffn_kernel,
        grid=(e, cap // tm),
        in_specs=[pl.BlockSpec((1, tm, d), lambda ei, i: (ei, i, 0)),
                  pl.BlockSpec((1, 1, d, ff), lambda ei, i: (layer, ei, 0, 0)),
                  pl.BlockSpec((1, 1, d, ff), lambda ei, i: (layer, ei, 0, 0)),
                  pl.BlockSpec((1, 1, ff, d), lambda ei, i: (layer, ei, 0, 0)),
                  pl.BlockSpec((1, 1, tm), lambda ei, i: (ei, 0, i))],
        out_specs=pl.BlockSpec((1, tm, d), lambda ei, i: (ei, i, 0)),
        out_shape=jax.ShapeDtypeStruct((e, cap, d), F32),
        compiler_params=_cparams(("parallel", "arbitrary")),
        name="expert_ffn",
    )(xg, wg, wu, wd, gate)


def _combine_kernel(offs_ref, x_ref, tok_hbm, z_hbm, o_ref, tokbuf, zbuf, sem, *, nsub):
    step = pl.program_id(0)
    blk0 = step * nsub
    c_lo = lax.shift_right_logical(offs_ref[blk0], SEG_ROWS_LOG2)
    c_hi = lax.shift_right_logical(offs_ref[blk0 + nsub] + (SEG_ROWS - 1), SEG_ROWS_LOG2)
    o_ref[...] = x_ref[...]

    def copies(c, slot):
        return (pltpu.make_async_copy(tok_hbm.at[c], tokbuf.at[slot], sem.at[0, slot]),
                pltpu.make_async_copy(z_hbm.at[pl.ds(pl.multiple_of(c * SEG_ROWS, SEG_ROWS), SEG_ROWS)],
                                      zbuf.at[slot], sem.at[1, slot]))

    for j in range(SEG_BUFS - 1):
        @pl.when(c_lo + j < c_hi)
        def _():
            for cp in copies(c_lo + j, j):
                cp.start()

    def body(c, carry):
        slot = lax.rem(c - c_lo, SEG_BUFS)
        for cp in copies(c, slot):
            cp.wait()
        ahead = c + (SEG_BUFS - 1)

        @pl.when(ahead < c_hi)
        def _():
            for cp in copies(ahead, lax.rem(ahead - c_lo, SEG_BUFS)):
                cp.start()

        tok = tokbuf[slot]
        z = zbuf[slot]
        zh = z.astype(BF16)
        zl = (z - zh.astype(F32)).astype(BF16)
        row0 = c * SEG_ROWS
        u_first, u_end = jnp.int32(0), jnp.int32(0)
        for u in range(nsub):
            u_first = u_first + (offs_ref[blk0 + u + 1] <= row0).astype(jnp.int32)
            u_end = u_end + (offs_ref[blk0 + u] < row0 + SEG_ROWS).astype(jnp.int32)

        def add_block(u, valid):
            base = jnp.where(valid, (blk0 + u) * SEG_TOK, -2 * SEG_TOK)
            tid = base + lax.broadcasted_iota(jnp.int32, (SEG_TOK, SEG_ROWS), 0)
            onehot = jnp.where(tok == tid, 1.0, 0.0).astype(BF16)
            upd = (jnp.dot(onehot, zh, preferred_element_type=F32)
                   + jnp.dot(onehot, zl, preferred_element_type=F32))
            start = pl.multiple_of(jnp.minimum(u, nsub - 1) * SEG_TOK, SEG_TOK)
            o_ref[pl.ds(start, SEG_TOK), :] += upd

        add_block(u_first, u_first < u_end)
        add_block(u_first + 1, u_first + 1 < u_end)

        def more(u, cr):
            add_block(u, True)
            return cr

        lax.fori_loop(u_first + 2, u_end, more, 0)
        return carry

    lax.fori_loop(c_lo, c_hi, body, 0)


def _combine(offs, x, tok, z):
    n, d = x.shape
    tn = _pick(n, SEG_STEP_TOK)
    return pl.pallas_call(
        functools.partial(_combine_kernel, nsub=tn // SEG_TOK),
        grid_spec=pltpu.PrefetchScalarGridSpec(
            num_scalar_prefetch=1,
            grid=(n // tn,),
            in_specs=[pl.BlockSpec((tn, d), lambda i, offs: (i, 0)),
                      pl.BlockSpec(memory_space=pl.ANY), pl.BlockSpec(memory_space=pl.ANY)],
            out_specs=pl.BlockSpec((tn, d), lambda i, offs: (i, 0)),
            scratch_shapes=[pltpu.VMEM((SEG_BUFS, 1, SEG_ROWS), jnp.int32),
                            pltpu.VMEM((SEG_BUFS, SEG_ROWS, d), F32),
                            pltpu.SemaphoreType.DMA((2, SEG_BUFS))]),
        out_shape=jax.ShapeDtypeStruct((n, d), F32),
        compiler_params=_cparams(("arbitrary",)),
        name="combine_segsum",
    )(offs, x, tok, z)


def _block_diag(hd):
    i = jnp.arange(MXU_DIM)
    return (i[:, None] // hd == i[None, :] // hd).astype(BF16)


def _rope_tables_std(seq, dim):
    half = dim // 2
    inv_freq = ROPE_THETA ** (-jnp.arange(half, dtype=F32) * 2.0 / dim)
    ang = jnp.arange(seq).astype(F32)[:, None] * inv_freq[None, :]
    cos, sin = jnp.cos(ang), jnp.sin(ang)
    reps = LANES // dim
    return (jnp.tile(jnp.concatenate([cos, cos], axis=-1), (1, reps)),
            jnp.tile(jnp.concatenate([-sin, sin], axis=-1), (1, reps)))


def _rope_tables_axial(seq):
    sub = HEAD_DIM // 2
    half = sub // 2
    inv_freq = ROPE_THETA ** (-jnp.arange(half, dtype=F32) * 2.0 / sub)
    n_rows = seq // GRID_W
    rows = jnp.repeat(jnp.arange(n_rows), GRID_W).astype(F32)
    cols = jnp.tile(jnp.arange(GRID_W), n_rows).astype(F32)
    ar, ac = rows[:, None] * inv_freq[None, :], cols[:, None] * inv_freq[None, :]
    cos = jnp.concatenate([jnp.cos(ar), jnp.cos(ar), jnp.cos(ac), jnp.cos(ac)], axis=-1)
    sin = jnp.concatenate([-jnp.sin(ar), jnp.sin(ar), -jnp.sin(ac), jnp.sin(ac)], axis=-1)
    return jnp.tile(cos, (1, 2)), jnp.tile(sin, (1, 2))


def _dilated_bias(band):
    t = ATT_BLK
    rel = jnp.arange(-band - 1, band + 2)[:, None, None] * t
    d = rel + jnp.arange(t)[None, :, None] - jnp.arange(t)[None, None, :]
    cnt = jnp.zeros(d.shape, F32)
    for window, dil in DILATED_BRANCHES:
        cnt = cnt + ((d % dil == 0) & (jnp.abs(d) <= window // 2)).astype(F32)
    return jnp.where(cnt > 0, jnp.log2(jnp.maximum(cnt, 1.0)), NEG_BIG)


def _tile_gain(g, reps):
    return jnp.tile(g.astype(F32), reps)[None, :]


def _pick(n, pref):
    t = min(n, pref)
    while n % t:
        t //= 2
    return t


def _mixer_a(x, b, s, g, w_qkv, q_g, k_g):
    nq = A_HEADS * HEAD_DIM
    nkv = A_KV_HEADS * HEAD_DIM
    wq, wk, wv = w_qkv[:, :nq], w_qkv[:, nq:nq + nkv], w_qkv[:, nq + nkv:]
    wk2 = jnp.repeat(wk.reshape(D_MODEL, A_KV_HEADS, 1, HEAD_DIM), 2, axis=2).reshape(D_MODEL, -1)
    w = jnp.concatenate([wq, wk2], axis=1).astype(BF16)
    nk = 2 * nkv
    gain = jnp.concatenate([_tile_gain(q_g, nq // HEAD_DIM), _tile_gain(k_g, nk // HEAD_DIM)], axis=1)
    cos, sin = _rope_tables_axial(s)
    q, k, vt = _qkv_proj(x, g, w, wv.T.astype(BF16), gain, cos, sin, _block_diag(HEAD_DIM), nq=nq, nk=nk,
                         half=HEAD_DIM // 4, q_scale=HEAD_DIM ** -0.5 * LOG2E, v_heads=A_KV_HEADS, dv=HEAD_DIM,
                         seq=s, tm=_pick(s, ROW_TILE))
    group_pairs = (A_HEADS // A_KV_HEADS) // 2
    return _attention(q.reshape(b, s, -1), k.reshape(b, s, -1), vt, mode="pair", qw=LANES, dv=HEAD_DIM, nv=1,
                      share=group_pairs, tq=_pick(s, ATT_TQ // group_pairs))


def _mixer_b(x, b, s, g, w_down, q_lat_g, kv_lat_g, w_uq, w_ukv, q_nope_g, q_rope_g, k_nope_g, k_rope_g):
    zpad = lambda a, n: jnp.concatenate([a, jnp.zeros(a.shape[:-1] + (n,), a.dtype)], axis=-1)
    w_kr = w_down[:, B_Q_LORA + B_KV_LORA:]
    wd = jnp.concatenate([w_down[:, :B_Q_LORA + B_KV_LORA], zpad(jnp.concatenate([w_kr, w_kr], axis=1), 2 * B_ROPE)],
                         axis=1).astype(BF16)
    uq = w_uq.reshape(B_Q_LORA, B_HEADS, B_NOPE + B_ROPE)
    uq_n = uq[:, :, :B_NOPE].reshape(B_Q_LORA, -1)
    uq_r = zpad(uq[:, :, B_NOPE:].reshape(B_Q_LORA, B_HEADS // 2, 2 * B_ROPE), LANES - 2 * B_ROPE).reshape(B_Q_LORA, -1)
    wuq = jnp.concatenate([uq_n, uq_r], axis=1).astype(BF16)
    ukv = w_ukv.reshape(B_KV_LORA, B_HEADS, B_NOPE + B_V)
    wuk = ukv[:, :, :B_NOPE].reshape(B_KV_LORA, -1).astype(BF16)
    wuvt = ukv[:, :, B_NOPE:].reshape(B_KV_LORA, -1).T.astype(BF16)
    gq = jnp.concatenate([_tile_gain(q_nope_g, B_HEADS), _tile_gain(q_rope_g, B_HEADS * B_NOPE // B_ROPE)], axis=1)
    gk = jnp.concatenate([_tile_gain(k_nope_g, B_HEADS), _tile_gain(k_rope_g, LANES // B_ROPE)], axis=1)
    cos, sin = _rope_tables_std(s, B_ROPE)
    q, k, vt = _mla_proj(x, g, wd, q_lat_g.astype(F32)[None, :], kv_lat_g.astype(F32)[None, :], wuq, wuk, wuvt,
                         _block_diag(B_NOPE), _block_diag(B_ROPE), gq, gk, cos, sin,
                         scale=(B_NOPE + B_ROPE) ** -0.5 * LOG2E, seq=s, tm=_pick(s, ROW_TILE))
    return _attention(q.reshape(b, s, -1), k.reshape(b, s, -1), vt, mode="pair", qw=2 * LANES, dv=B_V, nv=2,
                      share=0, tq=_pick(s, ATT_TQ))


def _mixer_c(x, b, s, g, w_qkv, q_g, k_g, lq1, lk1, lq2, lk2, subln_g, layer_idx):
    nq = 2 * C_HEADS * HEAD_DIM
    gain = jnp.concatenate([_tile_gain(q_g, nq // HEAD_DIM), _tile_gain(k_g, nq // HEAD_DIM)], axis=1)
    cos, sin = _rope_tables_std(s, HEAD_DIM)
    q, k, vt = _qkv_proj(x, g, w_qkv[:, :2 * nq].astype(BF16), w_qkv[:, 2 * nq:].T.astype(BF16), gain, cos, sin,
                         _block_diag(HEAD_DIM), nq=nq, nk=nq, half=HEAD_DIM // 2, q_scale=HEAD_DIM ** -0.5 * LOG2E,
                         v_heads=C_HEADS, dv=2 * HEAD_DIM, seq=s, tm=_pick(s, ROW_TILE))
    lam_init = LAMBDA_INIT_BASE - LAMBDA_INIT_AMP * math.exp(-LAMBDA_INIT_RATE * layer_idx)
    lam = (jnp.exp(jnp.sum(lq1.astype(F32) * lk1.astype(F32))) - jnp.exp(jnp.sum(lq2.astype(F32) * lk2.astype(F32)))
           + lam_init)
    lam_arr = jnp.stack([lam, jnp.asarray(lam_init, F32)]).reshape(1, 2).astype(F32)
    return _attention(q.reshape(b, s, -1), k.reshape(b, s, -1), vt, mode="diff", qw=LANES, dv=2 * HEAD_DIM, nv=1,
                      share=0, tq=_pick(s, ATT_TQ), lam=lam_arr,
                      subg=subln_g.astype(F32)[None, :])


def _mixer_d(x, b, s, g, w_qkv, q_g, k_g):
    nq = D_HEADS * HEAD_DIM
    gain = jnp.concatenate([_tile_gain(q_g, D_HEADS), _tile_gain(k_g, D_HEADS)], axis=1)
    cos, sin = _rope_tables_std(s, HEAD_DIM)
    q, k, vt = _qkv_proj(x, g, w_qkv[:, :2 * nq].astype(BF16), w_qkv[:, 2 * nq:].T.astype(BF16), gain, cos, sin,
                         _block_diag(HEAD_DIM), nq=nq, nk=nq, half=HEAD_DIM // 2, q_scale=HEAD_DIM ** -0.5 * LOG2E,
                         v_heads=D_HEADS, dv=HEAD_DIM, seq=s, tm=_pick(s, ROW_TILE))
    reach = max(w // 2 for w, _ in DILATED_BRANCHES)
    band = -(-reach // ATT_BLK)
    tq = _pick(s, ATT_TQ if ATT_TQ // ATT_BLK + 2 * band >= s // ATT_BLK else 2 * ATT_BLK)
    return _attention(q.reshape(b, s, -1), k.reshape(b, s, -1), vt, mode="pair", qw=LANES, dv=HEAD_DIM, nv=2,
                      share=0, tq=tq, bias=_dilated_bias(band), band=band)


def _ec_ffn(xnew, xn, aff, wg, wu, wd, layer):
    n, d = xnew.shape
    cap = (EC_CAPACITY * n) // N_EXPERTS
    gate, idx = lax.top_k(aff, cap)
    y = _expert_ffn(xn[idx], wg, wu, wd, gate[:, None, :], layer=layer, tm=_pick(cap, ROW_TILE))
    flat = idx.reshape(-1)
    tok_sorted, order = lax.sort_key_val(flat, jnp.arange(flat.size, dtype=jnp.int32))
    bounds = jnp.arange(0, n + 1, SEG_TOK, dtype=jnp.int32)
    offs = jnp.sum((tok_sorted[None, :] < bounds[:, None]).astype(jnp.int32), axis=1)
    return _combine(offs, xnew, tok_sorted.reshape(-1, 1, SEG_ROWS), y.reshape(-1, d)[order])


def kernel(x_prompt, x_sample, norm_mix_g, norm_ffn_g, a_w_qkv, a_q_norm_g, a_k_norm_g, a_w_o, b_w_down, b_q_lat_norm_g, b_kv_lat_norm_g, b_w_uq, b_w_ukv, b_q_nope_norm_g, b_q_rope_norm_g, b_k_nope_norm_g, b_k_rope_norm_g, b_w_o, c_w_qkv, c_q_norm_g, c_k_norm_g, c_lambda_q1, c_lambda_k1, c_lambda_q2, c_lambda_k2, c_subln_g, c_w_o, d_w_qkv, d_q_norm_g, d_k_norm_g, d_w_o, ec_w_router, ec_w_gate, ec_w_up, ec_w_down):
    depth = norm_mix_g.shape[0]
    wg_all, wu_all, wd_all = ec_w_gate.astype(BF16), ec_w_up.astype(BF16), ec_w_down.astype(BF16)

    def trunk(x3):
        b, s, d = x3.shape
        x = x3.reshape(b * s, d)
        for i in range(depth):
            m, j = i % N_MIXERS, i // N_MIXERS
            g = norm_mix_g[i].astype(F32)[None, :]
            if m == 0:
                o, w_o = _mixer_a(x, b, s, g, a_w_qkv[j], a_q_norm_g[j], a_k_norm_g[j]), a_w_o[j]
            elif m == 1:
                o = _mixer_b(x, b, s, g, b_w_down[j], b_q_lat_norm_g[j], b_kv_lat_norm_g[j], b_w_uq[j], b_w_ukv[j],
                             b_q_nope_norm_g[j], b_q_rope_norm_g[j], b_k_nope_norm_g[j], b_k_rope_norm_g[j])
                w_o = b_w_o[j]
            elif m == 2:
                o = _mixer_c(x, b, s, g, c_w_qkv[j], c_q_norm_g[j], c_k_norm_g[j], c_lambda_q1[j], c_lambda_k1[j],
                             c_lambda_q2[j], c_lambda_k2[j], c_subln_g[j], i)
                w_o = c_w_o[j]
            else:
                o, w_o = _mixer_d(x, b, s, g, d_w_qkv[j], d_q_norm_g[j], d_k_norm_g[j]), d_w_o[j]
            xnew, xn, aff = _oproj(o.reshape(b * s, -1), w_o.astype(BF16), x, norm_ffn_g[i].astype(F32)[None, :],
                                   ec_w_router[i].T.astype(BF16), tm=_pick(b * s, ROW_TILE))
            x = _ec_ffn(xnew, xn, aff, wg_all, wu_all, wd_all, i)
        return x.reshape(b, s, d)

    return (trunk(x_prompt), trunk(x_sample))
```
